```python
import math
import jax, jax.numpy as jnp
from jax import lax
import numpy as np

D_MODEL = 2048
BATCH = 4
SEQ = 4096
DEPTH = 4

N_A = DEPTH // 2
N_B = DEPTH - N_A
MEM_LEN = 256
RET_HEADS = 8
RET_DK = D_MODEL // 16
RET_DV = 2 * RET_DK
RET_CHUNK = 128
RET_THETA_BASE = 10000.0
DIL_CONFIG = ((128, 1), (512, 4), (2048, 16))
DIL_HEADS = 16
DIL_DH = D_MODEL // 16
DIL_BLOCK = 128
MEM_HEADS = 4
MEM_DH = D_MODEL // 8
EPS = 1e-6

RET_QK_W = RET_HEADS * RET_DK
RET_V_W = RET_HEADS * RET_DV
DIL_W = DIL_HEADS * DIL_DH
MEM_W = MEM_HEADS * MEM_DH
MIX_W_A = RET_V_W + MEM_W
MIX_W_B = DIL_W + MEM_W
IN_W_A = 2 * RET_QK_W + 2 * RET_V_W + 2 * MEM_W
IN_W_B = len(DIL_CONFIG) * DIL_W + DIL_W + 2 * MEM_W
KV_W = 2 * len(DIL_CONFIG) * DIL_W

kernel_name = "yoco_retention_dilated_hybrid"


def split_cols(a, widths):
    idx, acc = [], 0
    for w in widths[:-1]:
        acc += w
        idx.append(acc)
    return jnp.split(a, idx, axis=-1)


def rmsnorm(x, g):
    xf = x.astype(jnp.float32)
    y = xf * lax.rsqrt(jnp.mean(xf * xf, axis=-1, keepdims=True) + EPS) * g.astype(jnp.float32)
    return y.astype(x.dtype)


def rotary(t):
    S, half = t.shape[1], t.shape[-1] // 2
    inv = 1.0 / (RET_THETA_BASE ** jnp.linspace(0.0, 1.0, half, dtype=jnp.float32))
    ang = jnp.arange(S, dtype=jnp.float32)[:, None] * inv[None, :]
    cos = jnp.cos(ang)[None, :, None, :]
    sin = jnp.sin(ang)[None, :, None, :]
    tf = t.astype(jnp.float32)
    t1, t2 = tf[..., :half], tf[..., half:]
    return jnp.concatenate([t1 * cos - t2 * sin, t1 * sin + t2 * cos], axis=-1)


def retention(q, k, v):
    B, S, H, Dk = q.shape
    Dv = v.shape[-1]
    C = math.gcd(S, RET_CHUNK)
    N = S // C
    log_g = jnp.log1p(-(2.0 ** (-5.0 - jnp.arange(H, dtype=jnp.float32))))
    qc = q.reshape(B, N, C, H, Dk)
    kc = k.reshape(B, N, C, H, Dk)
    vc = v.astype(jnp.float32).reshape(B, N, C, H, Dv)
    pos = jnp.arange(C, dtype=jnp.float32)
    rel = pos[:, None] - pos[None, :]
    decay_in = jnp.where(rel >= 0, jnp.exp(jnp.maximum(rel, 0.0)[None] * log_g[:, None, None]), 0.0)
    xi = jnp.exp((pos[None, :] + 1.0) * log_g[:, None])
    zeta = jnp.exp((C - 1.0 - pos[None, :]) * log_g[:, None])
    g_chunk = jnp.exp(C * log_g)
    s = jnp.einsum('bnqhd,bnkhd->bnhqk', qc, kc) * decay_in[None, None]
    o_in = jnp.einsum('bnhqk,bnkhe->bnqhe', s, vc)
    U = jnp.einsum('bnkhd,hk,bnkhe->bnhde', kc, zeta, vc)

    def step(R, U_n):
        return g_chunk[None, :, None, None] * R + U_n, R

    _, R_prev = lax.scan(step, jnp.zeros_like(U[:, 0]), jnp.moveaxis(U, 1, 0))
    cross = jnp.einsum('bnqhd,nbhde->bnqhe', qc, R_prev) * xi.T[None, None, :, :, None]
    return (o_in + cross).reshape(B, S, H, Dv)


def memory_branch(mq, gate, mem_n, w_mem_kv):
    B, S, _ = mq.shape
    M = mem_n.shape[1]
    mk, mv = split_cols(mem_n @ w_mem_kv, [MEM_W, MEM_W])
    q = mq.reshape(B, S, MEM_HEADS, MEM_DH)
    k = mk.reshape(B, M, MEM_HEADS, MEM_DH)
    v = mv.reshape(B, M, MEM_HEADS, MEM_DH)
    s = jnp.einsum('bshd,bmhd->bhsm', q, k).astype(jnp.float32) * (MEM_DH ** -0.5)
    p = jax.nn.softmax(s, axis=-1)
    o = jnp.einsum('bhsm,bmhd->bshd', p.astype(v.dtype), v).reshape(B, S, MEM_W)
    return o * jax.nn.silu(gate)


def dilated_group(q, k, v, dil, span):
    B, S, H, D = q.shape
    n = S // dil
    Qb = math.gcd(n, DIL_BLOCK)
    nb = n // Qb

    def to_res(t):
        return t.reshape(B, n, dil, H, D).transpose(0, 2, 1, 3, 4)

    qr = to_res(q).reshape(B, dil, nb, Qb, H, D)
    pad = ((0, 0), (0, 0), (span, 0), (0, 0), (0, 0))
    kr = jnp.pad(to_res(k), pad)
    vr = jnp.pad(to_res(v), pad)
    idx = jnp.arange(nb)[:, None] * Qb + jnp.arange(Qb + span)[None, :]
    kb = kr[:, :, idx]
    vb = vr[:, :, idx]
    dist = jnp.arange(Qb)[:, None] + span - jnp.arange(Qb + span)[None, :]
    band = (dist >= 0) & (dist <= span)
    mask = band[None] & ((idx - span)[:, None, :] >= 0)
    s = jnp.einsum('bgnqhd,bgnkhd->bgnhqk', qr, kb).astype(jnp.float32) * (D ** -0.5)
    s = jnp.where(mask[None, None, :, None], s, -1e30)
    m = jnp.max(s, axis=-1, keepdims=True)
    p = jnp.exp(s - m)
    l = jnp.sum(p, axis=-1)
    o = jnp.einsum('bgnhqk,bgnkhd->bgnqhd', p, vb.astype(jnp.float32))
    o = o / jnp.transpose(l, (0, 1, 2, 4, 3))[..., None]
    lse = jnp.transpose(m[..., 0] + jnp.log(l), (0, 1, 2, 4, 3))
    o = o.reshape(B, dil, n, H, D).transpose(0, 2, 1, 3, 4).reshape(B, S, H, D)
    lse = lse.reshape(B, dil, n, H).transpose(0, 2, 1, 3).reshape(B, S, H)
    return o, lse


def retention_layer(x, g, w_in, w_mem_kv, w_out, mem_n):
    B, S, _ = x.shape
    h = rmsnorm(x, g)
    q, k, v, gate_r, mq, gate_m = split_cols(h @ w_in, [RET_QK_W, RET_QK_W, RET_V_W, RET_V_W, MEM_W, MEM_W])
    q = rotary(q.reshape(B, S, RET_HEADS, RET_DK))
    k = rotary(k.reshape(B, S, RET_HEADS, RET_DK)) * (RET_DK ** -0.5)
    o = retention(q, k, v.reshape(B, S, RET_HEADS, RET_DV))
    o = o * lax.rsqrt(jnp.mean(o * o, axis=-1, keepdims=True) + EPS)
    o_ret = o.reshape(B, S, RET_V_W).astype(x.dtype) * jax.nn.silu(gate_r)
    o_mem = memory_branch(mq, gate_m, mem_n, w_mem_kv)
    return x + jnp.concatenate([o_ret, o_mem], axis=-1) @ w_out


def shared_kv(x, g, w_kv):
    B, S, _ = x.shape
    parts = split_cols(rmsnorm(x, g) @ w_kv, [DIL_W] * (2 * len(DIL_CONFIG)))
    return [t.reshape(B, S, DIL_HEADS, DIL_DH) for t in parts]


def dilated_layer(x, g, w_in, w_mem_kv, w_out, mem_n, kv):
    B, S, _ = x.shape
    h = rmsnorm(x, g)
    n_g = len(DIL_CONFIG)
    parts = split_cols(h @ w_in, [DIL_W] * n_g + [DIL_W, MEM_W, MEM_W])
    gate_d, mq, gate_m = parts[n_g], parts[n_g + 1], parts[n_g + 2]
    outs, lses = [], []
    for gi, (window, dil) in enumerate(DIL_CONFIG):
        q = parts[gi].reshape(B, S, DIL_HEADS, DIL_DH)
        o, lse = dilated_group(q, kv[2 * gi], kv[2 * gi + 1], dil, window // dil)
        outs.append(o)
        lses.append(lse)
    alpha = jax.nn.softmax(jnp.stack(lses, axis=0), axis=0)
    o = jnp.sum(alpha[..., None] * jnp.stack(outs, axis=0), axis=0)
    o_dil = o.reshape(B, S, DIL_W).astype(x.dtype) * jax.nn.silu(gate_d)
    o_mem = memory_branch(mq, gate_m, mem_n, w_mem_kv)
    return x + jnp.concatenate([o_dil, o_mem], axis=-1) @ w_out


def setup_inputs(seed: int = 0) -> dict:
    key = jax.random.key(seed)
    ks = jax.random.split(key, 14)
    f32 = jnp.float32

    def w(k, shape, fan_in):
        return jax.random.normal(k, shape, f32) * (fan_in ** -0.5)

    def gain(k, shape):
        return 1.0 + 0.02 * jax.random.normal(k, shape, f32)

    return {
        "x": jax.random.normal(ks[0], (BATCH, SEQ, D_MODEL), f32),
        "mem": jax.random.normal(ks[1], (BATCH, MEM_LEN, D_MODEL), f32),
        "norm_a": gain(ks[2], (N_A, D_MODEL)),
        "w_in_a": w(ks[3], (N_A, D_MODEL, IN_W_A), D_MODEL),
        "w_out_a": w(ks[4], (N_A, MIX_W_A, D_MODEL), MIX_W_A),
        "norm_b": gain(ks[5], (N_B, D_MODEL)),
        "w_in_b": w(ks[6], (N_B, D_MODEL, IN_W_B), D_MODEL),
        "w_out_b": w(ks[7], (N_B, MIX_W_B, D_MODEL), MIX_W_B),
        "w_mem_kv": w(ks[8], (DEPTH, D_MODEL, 2 * MEM_W), D_MODEL),
        "mem_norm_g": gain(ks[9], (D_MODEL,)),
        "kv_norm_g": gain(ks[10], (D_MODEL,)),
        "w_kv": w(ks[11], (D_MODEL, KV_W), D_MODEL),
        "final_norm_g": gain(ks[12], (D_MODEL,)),
    }


def reference(x, mem, norm_a, w_in_a, w_out_a, norm_b, w_in_b, w_out_b, w_mem_kv, mem_norm_g, kv_norm_g, w_kv, final_norm_g):
    mem_n = rmsnorm(mem, mem_norm_g)
    kv = None
    for layer in range(DEPTH):
        if layer < N_A:
            x = retention_layer(x, norm_a[layer], w_in_a[layer], w_mem_kv[layer], w_out_a[layer], mem_n)
            if layer == N_A - 1:
                kv = shared_kv(x, kv_norm_g, w_kv)
        else:
            j = layer - N_A
            x = dilated_layer(x, norm_b[j], w_in_b[j], w_mem_kv[layer], w_out_b[j], mem_n, kv)
    return rmsnorm(x, final_norm_g)
```

```python
import functools
import math

import jax
import jax.numpy as jnp
from jax import lax
from jax.experimental import pallas as pl
from jax.experimental.pallas import tpu as pltpu

F32 = jnp.float32
BF16 = jnp.bfloat16

D_MODEL = 2048
DEPTH = 4
N_A = DEPTH // 2
MEM_LEN = 256
RET_HEADS = 8
RET_DK = D_MODEL // 16
RET_DV = 2 * RET_DK
RET_CHUNK = 128
RET_THETA_BASE = 10000.0
DIL_CONFIG = ((128, 1), (512, 4), (2048, 16))
DIL_HEADS = 16
DIL_DH = D_MODEL // 16
DIL_BLOCK = 128
MEM_HEADS = 4
MEM_DH = D_MODEL // 8
EPS = 1e-6

RET_QK_W = RET_HEADS * RET_DK
RET_V_W = RET_HEADS * RET_DV
DIL_W = DIL_HEADS * DIL_DH
MEM_W = MEM_HEADS * MEM_DH
IN_W_A = 2 * RET_QK_W + 2 * RET_V_W + 2 * MEM_W
IN_W_B = len(DIL_CONFIG) * DIL_W + DIL_W + 2 * MEM_W
KV_W = 2 * len(DIL_CONFIG) * DIL_W

LANES = 128
VMEM_LIMIT_BYTES = 56 * 2**20
NORM_ROW_CHUNK = 64
PROJ_TM, PROJ_TN = 1024, 1024
RET_TQ = 512
MEM_TQ = 512
FINAL_TM = 512


def _params(semantics):
    return pltpu.CompilerParams(dimension_semantics=semantics, vmem_limit_bytes=VMEM_LIMIT_BYTES)


def _dot(a, b):
    return jnp.dot(a, b, preferred_element_type=F32)


def _dot_nt(a, b):
    return lax.dot_general(a, b, (((1,), (1,)), ((), ())), preferred_element_type=F32)


def _silu(x):
    return x * (1.0 / (1.0 + jnp.exp(-x)))


def _norm_matmul_kernel(x_ref, g_ref, w_ref, o_ref, xn_ref):
    @pl.when(pl.program_id(1) == 0)
    def _normalize():
        def body(c, carry):
            r = pl.multiple_of(c * NORM_ROW_CHUNK, NORM_ROW_CHUNK)
            xc = x_ref[pl.ds(r, NORM_ROW_CHUNK), :]
            ms = jnp.mean(xc * xc, axis=-1, keepdims=True)
            xn_ref[pl.ds(r, NORM_ROW_CHUNK), :] = (xc * lax.rsqrt(ms + EPS) * g_ref[...]).astype(BF16)
            return carry

        lax.fori_loop(0, x_ref.shape[0] // NORM_ROW_CHUNK, body, 0)

    o_ref[...] = _dot(xn_ref[...], w_ref[...]).astype(o_ref.dtype)


def norm_matmul(x, g, g_idx, w, w_idx, *, name):
    M, K = x.shape
    N = w.shape[-1]
    tm, tn = min(PROJ_TM, M), min(PROJ_TN, N)
    return pl.pallas_call(
        _norm_matmul_kernel,
        grid=(M // tm, N // tn),
        in_specs=[
            pl.BlockSpec((tm, K), lambda i, j: (i, 0)),
            pl.BlockSpec((None, 1, K), lambda i, j: (g_idx, 0, 0)),
            pl.BlockSpec((None, K, tn), lambda i, j: (w_idx, 0, j)),
        ],
        out_specs=pl.BlockSpec((tm, tn), lambda i, j: (i, j)),
        out_shape=jax.ShapeDtypeStruct((M, N), BF16),
        scratch_shapes=[pltpu.VMEM((tm, K), BF16)],
        compiler_params=_params(("parallel", "arbitrary")),
        name=name,
    )(x, g, w)


def _out_proj_kernel(om_ref, omem_ref, w_ref, x_ref, o_ref):
    k_main = om_ref.shape[1]
    acc = _dot(om_ref[...], w_ref[:k_main, :]) + _dot(omem_ref[...], w_ref[k_main:, :])
    o_ref[...] = x_ref[...] + acc


def out_proj(o_main, o_mem, w, layer, x, *, name):
    M, k_main = o_main.shape
    k_mem = o_mem.shape[1]
    N = w.shape[-1]
    tm, tn = PROJ_TM, PROJ_TN
    return pl.pallas_call(
        _out_proj_kernel,
        grid=(M // tm, N // tn),
        in_specs=[
            pl.BlockSpec((tm, k_main), lambda i, j: (i, 0)),
            pl.BlockSpec((tm, k_mem), lambda i, j: (i, 0)),
            pl.BlockSpec((None, k_main + k_mem, tn), lambda i, j: (layer, 0, j)),
            pl.BlockSpec((tm, tn), lambda i, j: (i, j)),
        ],
        out_specs=pl.BlockSpec((tm, tn), lambda i, j: (i, j)),
        out_shape=jax.ShapeDtypeStruct((M, N), F32),
        compiler_params=_params(("parallel", "parallel")),
        name=name,
    )(o_main, o_mem, w, x)


def _rmsnorm_kernel(x_ref, g_ref, o_ref):
    def body(c, carry):
        r = pl.multiple_of(c * NORM_ROW_CHUNK, NORM_ROW_CHUNK)
        xc = x_ref[pl.ds(r, NORM_ROW_CHUNK), :]
        ms = jnp.mean(xc * xc, axis=-1, keepdims=True)
        o_ref[pl.ds(r, NORM_ROW_CHUNK), :] = xc * lax.rsqrt(ms + EPS) * g_ref[...]
        return carry

    lax.fori_loop(0, x_ref.shape[0] // NORM_ROW_CHUNK, body, 0)


def rmsnorm_rows(x, g):
    M, K = x.shape
    tm = FINAL_TM
    return pl.pallas_call(
        _rmsnorm_kernel,
        grid=(M // tm,),
        in_specs=[pl.BlockSpec((tm, K), lambda i: (i, 0)), pl.BlockSpec((1, K), lambda i: (0, 0))],
        out_specs=pl.BlockSpec((tm, K), lambda i: (i, 0)),
        out_shape=jax.ShapeDtypeStruct((M, K), F32),
        compiler_params=_params(("parallel",)),
        name="final_rmsnorm",
    )(x, g)


def _retention_kernel(g_ref, q_ref, k_ref, v_ref, gate_ref, cos_ref, sin_ref, decay_ref, xi_ref, zeta_ref,
                      o_ref, state_ref):
    @pl.when(pl.program_id(1) == 0)
    def _reset():
        state_ref[...] = jnp.zeros_like(state_ref)

    def rotate(t, cos, sin):
        return t * cos + pltpu.roll(t, RET_DK // 2, axis=1) * sin

    def chunk_body(c, carry):
        rows = pl.ds(pl.multiple_of(c * RET_CHUNK, RET_CHUNK), RET_CHUNK)
        cos = cos_ref[rows, :]
        sin = sin_ref[rows, :]
        for h in range(RET_HEADS):
            qk_cols = slice(h * RET_DK, (h + 1) * RET_DK)
            v_cols = slice(h * RET_DV, (h + 1) * RET_DV)
            q = rotate(q_ref[rows, qk_cols].astype(F32), cos, sin)
            k = rotate(k_ref[rows, qk_cols].astype(F32), cos, sin) * (RET_DK ** -0.5)
            v = v_ref[rows, v_cols]
            qb = q.astype(BF16)
            s = _dot_nt(qb, k.astype(BF16)) * decay_ref[h]
            state = state_ref[h]
            o = _dot(s.astype(BF16), v) + _dot(qb, state.astype(BF16)) * xi_ref[h]
            kz_t = (k * zeta_ref[h]).T.astype(BF16)
            state_ref[h] = g_ref[h] * state + _dot(kz_t, v)
            o = o * lax.rsqrt(jnp.mean(o * o, axis=-1, keepdims=True) + EPS)
            o_ref[rows, v_cols] = (o * _silu(gate_ref[rows, v_cols].astype(F32))).astype(o_ref.dtype)
        return carry

    lax.fori_loop(0, q_ref.shape[0] // RET_CHUNK, chunk_body, 0)


def _rotary_tables(S):
    half = RET_DK // 2
    inv = 1.0 / (RET_THETA_BASE ** jnp.linspace(0.0, 1.0, half, dtype=F32))
    ang = jnp.arange(S, dtype=F32)[:, None] * inv[None, :]
    cos, sin = jnp.cos(ang), jnp.sin(ang)
    return jnp.concatenate([cos, cos], axis=-1), jnp.concatenate([-sin, sin], axis=-1)


def _decay_tables():
    H, C = RET_HEADS, RET_CHUNK
    log_g = jnp.log1p(-(2.0 ** (-5.0 - jnp.arange(H, dtype=F32))))
    pos = jnp.arange(C, dtype=F32)
    rel = pos[:, None] - pos[None, :]
    decay_in = jnp.where(rel >= 0, jnp.exp(jnp.maximum(rel, 0.0)[None] * log_g[:, None, None]), 0.0)
    xi = jnp.exp((pos[None, :] + 1.0) * log_g[:, None])
    zeta = jnp.exp((C - 1.0 - pos[None, :]) * log_g[:, None])
    g_chunk = jnp.exp(C * log_g)
    return g_chunk, decay_in, xi[:, :, None], zeta[:, :, None]


def retention_gated(proj, B, S):
    M = B * S
    tq = RET_TQ
    spb = S // tq
    g_chunk, decay_in, xi, zeta = _decay_tables()
    cos, sin = _rotary_tables(S)
    k_blk = 1
    v_blk = 2 * RET_QK_W // RET_V_W
    gate_blk = v_blk + 1
    row = lambda b, i: b * spb + i
    const3 = lambda b, i: (0, 0, 0)
    return pl.pallas_call(
        _retention_kernel,
        grid=(B, spb),
        in_specs=[
            pl.BlockSpec(memory_space=pltpu.SMEM),
            pl.BlockSpec((tq, RET_QK_W), lambda b, i: (row(b, i), 0)),
            pl.BlockSpec((tq, RET_QK_W), lambda b, i: (row(b, i), k_blk)),
            pl.BlockSpec((tq, RET_V_W), lambda b, i: (row(b, i), v_blk)),
            pl.BlockSpec((tq, RET_V_W), lambda b, i: (row(b, i), gate_blk)),
            pl.BlockSpec((tq, RET_DK), lambda b, i: (i, 0)),
            pl.BlockSpec((tq, RET_DK), lambda b, i: (i, 0)),
            pl.BlockSpec((RET_HEADS, RET_CHUNK, RET_CHUNK), const3),
            pl.BlockSpec((RET_HEADS, RET_CHUNK, 1), const3),
            pl.BlockSpec((RET_HEADS, RET_CHUNK, 1), const3),
        ],
        out_specs=pl.BlockSpec((tq, RET_V_W), lambda b, i: (row(b, i), 0)),
        out_shape=jax.ShapeDtypeStruct((M, RET_V_W), BF16),
        scratch_shapes=[pltpu.VMEM((RET_HEADS, RET_DK, RET_DV), F32)],
        compiler_params=_params(("parallel", "arbitrary")),
        name="retention",
    )(g_chunk, proj, proj, proj, proj, cos, sin, decay_in, xi, zeta)


def _mem_attn_kernel(q_ref, gate_ref, k_ref, v_ref, o_ref):
    for h in range(MEM_HEADS):
        cols = slice(h * MEM_DH, (h + 1) * MEM_DH)
        s = _dot_nt(q_ref[:, cols], k_ref[:, cols]) * (MEM_DH ** -0.5)
        p = jnp.exp(s - jnp.max(s, axis=-1, keepdims=True))
        l = jnp.sum(p, axis=-1, keepdims=True)
        o = _dot(p.astype(BF16), v_ref[:, cols]) * (1.0 / l)
        o_ref[:, cols] = (o * _silu(gate_ref[:, cols].astype(F32))).astype(o_ref.dtype)


def mem_attention_gated(proj, mem_kv, B, S, q_blk):
    M = B * S
    tq = MEM_TQ
    spb = S // tq
    row = lambda b, i: b * spb + i
    return pl.pallas_call(
        _mem_attn_kernel,
        grid=(B, spb),
        in_specs=[
            pl.BlockSpec((tq, MEM_W), lambda b, i: (row(b, i), q_blk)),
            pl.BlockSpec((tq, MEM_W), lambda b, i: (row(b, i), q_blk + 1)),
            pl.BlockSpec((MEM_LEN, MEM_W), lambda b, i: (b, 0)),
            pl.BlockSpec((MEM_LEN, MEM_W), lambda b, i: (b, 1)),
        ],
        out_specs=pl.BlockSpec((tq, MEM_W), lambda b, i: (row(b, i), 0)),
        out_shape=jax.ShapeDtypeStruct((M, MEM_W), BF16),
        compiler_params=_params(("parallel", "parallel")),
        name="mem_attention",
    )(proj, proj, mem_kv, mem_kv)


def _dilated_kernel(*refs, has_prev, final):
    refs = list(refs)
    q_ref, k_ref, v_ref = refs[:3]
    pos = 3
    if has_prev:
        oprev_ref, lprev_ref = refs[pos:pos + 2]
        pos += 2
    if final:
        gate_ref = refs[pos]
        pos += 1
    o_ref = refs[pos]
    pos += 1
    if not final:
        lse_ref = refs[pos]
        pos += 1
    kprev_ref, vprev_ref = refs[pos:pos + 2]

    j = pl.program_id(2)

    @pl.when(j == 0)
    def _reset():
        kprev_ref[...] = jnp.zeros_like(kprev_ref)
        vprev_ref[...] = jnp.zeros_like(vprev_ref)

    qi = lax.broadcasted_iota(jnp.int32, (DIL_BLOCK, DIL_BLOCK), 0)
    kk = lax.broadcasted_iota(jnp.int32, (DIL_BLOCK, DIL_BLOCK), 1)
    mask_cur = kk <= qi
    mask_prev = kk >= qi + jnp.where(j > 0, 0, DIL_BLOCK)
    lane = lax.broadcasted_iota(jnp.int32, (DIL_BLOCK, LANES), 1)
    lse_tile = jnp.zeros((DIL_BLOCK, LANES), F32)
    scale = DIL_DH ** -0.5

    for h in range(DIL_HEADS):
        cols = slice(h * DIL_DH, (h + 1) * DIL_DH)
        q = q_ref[:, cols]
        s_p = jnp.where(mask_prev, _dot_nt(q, kprev_ref[:, cols]) * scale, -1e30)
        s_c = jnp.where(mask_cur, _dot_nt(q, k_ref[:, cols]) * scale, -1e30)
        m = jnp.maximum(jnp.max(s_p, axis=-1, keepdims=True), jnp.max(s_c, axis=-1, keepdims=True))
        p_p = jnp.exp(s_p - m)
        p_c = jnp.exp(s_c - m)
        l = jnp.sum(p_p, axis=-1, keepdims=True) + jnp.sum(p_c, axis=-1, keepdims=True)
        acc = _dot(p_p.astype(BF16), vprev_ref[:, cols]) + _dot(p_c.astype(BF16), v_ref[:, cols])
        o = acc * (1.0 / l)
        lse = m + jnp.log(l)
        if has_prev:
            lse_a = lprev_ref[:, h:h + 1]
            mm = jnp.maximum(lse_a, lse)
            w_a = jnp.exp(lse_a - mm)
            w_b = jnp.exp(lse - mm)
            den = w_a + w_b
            o = (w_a * oprev_ref[:, cols].astype(F32) + w_b * o) * (1.0 / den)
            lse = mm + jnp.log(den)
        if final:
            o = o * _silu(gate_ref[:, cols].astype(F32))
        else:
            lse_tile = jnp.where(lane == h, lse, lse_tile)
        o_ref[:, cols] = o.astype(o_ref.dtype)

    if not final:
        lse_ref[...] = lse_tile
    kprev_ref[...] = k_ref[...]
    vprev_ref[...] = v_ref[...]


def dilated_group_step(proj, kv, group, B, S, prev=None, final=False):
    _, dil = DIL_CONFIG[group]
    n = S // dil
    nb = n // DIL_BLOCK
    n_proj = IN_W_B // DIL_W
    n_kv = KV_W // DIL_W
    gate_blk = len(DIL_CONFIG)
    blk = (None, DIL_BLOCK, DIL_W)
    lse_blk = (None, DIL_BLOCK, LANES)
    tok = lambda b, r, j: (b, j, r)
    operands = [proj.reshape(B, n, dil * IN_W_B), kv.reshape(B, n, dil * KV_W), kv.reshape(B, n, dil * KV_W)]
    in_specs = [
        pl.BlockSpec(blk, lambda b, r, j: (b, j, r * n_proj + group)),
        pl.BlockSpec(blk, lambda b, r, j: (b, j, r * n_kv + 2 * group)),
        pl.BlockSpec(blk, lambda b, r, j: (b, j, r * n_kv + 2 * group + 1)),
    ]
    if prev is not None:
        o_prev, lse_prev = prev
        operands += [o_prev.reshape(B, n, dil * DIL_W), lse_prev.reshape(B, n, dil * LANES)]
        in_specs += [pl.BlockSpec(blk, tok), pl.BlockSpec(lse_blk, tok)]
    if final:
        operands.append(proj.reshape(B, n, dil * IN_W_B))
        in_specs.append(pl.BlockSpec(blk, lambda b, r, j: (b, j, r * n_proj + gate_blk)))
    out_shape = [jax.ShapeDtypeStruct((B, n, dil * DIL_W), BF16)]
    out_specs = [pl.BlockSpec(blk, tok)]
    if not final:
        out_shape.append(jax.ShapeDtypeStruct((B, n, dil * LANES), F32))
        out_specs.append(pl.BlockSpec(lse_blk, tok))
    outs = pl.pallas_call(
        functools.partial(_dilated_kernel, has_prev=prev is not None, final=final),
        grid=(B, dil, nb),
        in_specs=in_specs,
        out_specs=out_specs,
        out_shape=out_shape,
        scratch_shapes=[pltpu.VMEM((DIL_BLOCK, DIL_W), BF16), pltpu.VMEM((DIL_BLOCK, DIL_W), BF16)],
        compiler_params=_params(("arbitrary", "arbitrary", "arbitrary")),
        name=f"dilated_group{group}",
    )(*operands)
    o = outs[0].reshape(B * S, DIL_W)
    if final:
        return o
    return o, outs[1].reshape(B * S, LANES)


def dilated_gated(proj, kv, B, S):
    state = None
    last = len(DIL_CONFIG) - 1
    for group in range(last):
        state = dilated_group_step(proj, kv, group, B, S, prev=state)
    return dilated_group_step(proj, kv, last, B, S, prev=state, final=True)


def kernel(x, mem, norm_a, w_in_a, w_out_a, norm_b, w_in_b, w_out_b, w_mem_kv, mem_norm_g, kv_norm_g, w_kv, final_norm_g):
    B, S, D = x.shape
    xr = x.reshape(B * S, D)
    mem_r = mem.reshape(B * mem.shape[1], D)

    w_in_a, w_out_a, w_in_b, w_out_b, w_mem_kv = (t.astype(BF16) for t in (w_in_a, w_out_a, w_in_b, w_out_b, w_mem_kv))
    w_kv = w_kv.astype(BF16)[None]
    norm_a = norm_a[:, None, :]
    norm_b = norm_b[:, None, :]
    mem_g = mem_norm_g[None, None, :]
    kv_g = kv_norm_g[None, None, :]

    kv = None
    for layer in range(DEPTH):
        mem_kv = norm_matmul(mem_r, mem_g, 0, w_mem_kv, layer, name=f"mem_kv_proj{layer}")
        if layer < N_A:
            proj = norm_matmul(xr, norm_a, layer, w_in_a, layer, name=f"in_proj_a{layer}")
            o_main = retention_gated(proj, B, S)
            o_mem = mem_attention_gated(proj, mem_kv, B, S, (IN_W_A - 2 * MEM_W) // MEM_W)
            xr = out_proj(o_main, o_mem, w_out_a, layer, xr, name=f"out_proj_a{layer}")
            if layer == N_A - 1:
                kv = norm_matmul(xr, kv_g, 0, w_kv, 0, name="shared_kv_proj")
        else:
            jb = layer - N_A
            proj = norm_matmul(xr, norm_b, jb, w_in_b, jb, name=f"in_proj_b{jb}")
            o_main = dilated_gated(proj, kv, B, S)
            o_mem = mem_attention_gated(proj, mem_kv, B, S, (IN_W_B - 2 * MEM_W) // MEM_W)
            xr = out_proj(o_main, o_mem, w_out_b, jb, xr, name=f"out_proj_b{jb}")
    return rmsnorm_rows(xr, final_norm_g[None, :]).reshape(B, S, D)
```

```python
import functools
import math

import jax
import jax.numpy as jnp
from jax import lax
from jax.experimental import pallas as pl
from jax.experimental.pallas import tpu as pltpu

F32 = jnp.float32
BF16 = jnp.bfloat16

D_MODEL = 2048
DEPTH = 4
N_A = DEPTH // 2
MEM_LEN = 256
RET_HEADS = 8
RET_DK = D_MODEL // 16
RET_DV = 2 * RET_DK
RET_CHUNK = 128
RET_THETA_BASE = 10000.0
DIL_CONFIG = ((128, 1), (512, 4), (2048, 16))
DIL_HEADS = 16
DIL_DH = D_MODEL // 16
DIL_BLOCK = 128
MEM_HEADS = 4
MEM_DH = D_MODEL // 8
EPS = 1e-6

RET_QK_W = RET_HEADS * RET_DK
RET_V_W = RET_HEADS * RET_DV
DIL_W = DIL_HEADS * DIL_DH
MEM_W = MEM_HEADS * MEM_DH
IN_W_A = 2 * RET_QK_W + 2 * RET_V_W + 2 * MEM_W
IN_W_B = len(DIL_CONFIG) * DIL_W + DIL_W + 2 * MEM_W
KV_W = 2 * len(DIL_CONFIG) * DIL_W

LANES = 128
VMEM_LIMIT_BYTES = 56 * 2**20
NORM_ROW_CHUNK = 64
PROJ_TM, PROJ_TN = 1024, 1024
RET_TQ = 512
MEM_TQ = 512
FINAL_TM = 512


def _params(semantics):
    return pltpu.CompilerParams(dimension_semantics=semantics, vmem_limit_bytes=VMEM_LIMIT_BYTES)


def _dot(a, b):
    return jnp.dot(a, b, preferred_element_type=F32)


def _dot_nt(a, b):
    return lax.dot_general(a, b, (((1,), (1,)), ((), ())), preferred_element_type=F32)


def _silu(x):
    return x * (1.0 / (1.0 + jnp.exp(-x)))


def _norm_matmul_kernel(x_ref, g_ref, w_ref, o_ref, xn_ref):
    @pl.when(pl.program_id(1) == 0)
    def _normalize():
        def body(c, carry):
            r = pl.multiple_of(c * NORM_ROW_CHUNK, NORM_ROW_CHUNK)
            xc = x_ref[pl.ds(r, NORM_ROW_CHUNK), :]
            ms = jnp.mean(xc * xc, axis=-1, keepdims=True)
            xn_ref[pl.ds(r, NORM_ROW_CHUNK), :] = (xc * lax.rsqrt(ms + EPS) * g_ref[...]).astype(BF16)
            return carry

        lax.fori_loop(0, x_ref.shape[0] // NORM_ROW_CHUNK, body, 0)

    o_ref[...] = _dot(xn_ref[...], w_ref[...]).astype(o_ref.dtype)


def norm_matmul(x, g, g_idx, w, w_idx, *, name):
    M, K = x.shape
    N = w.shape[-1]
    tm, tn = min(PROJ_TM, M), min(PROJ_TN, N)
    return pl.pallas_call(
        _norm_matmul_kernel,
        grid=(M // tm, N // tn),
        in_specs=[
            pl.BlockSpec((tm, K), lambda i, j: (i, 0)),
            pl.BlockSpec((None, 1, K), lambda i, j: (g_idx, 0, 0)),
            pl.BlockSpec((None, K, tn), lambda i, j: (w_idx, 0, j)),
        ],
        out_specs=pl.BlockSpec((tm, tn), lambda i, j: (i, j)),
        out_shape=jax.ShapeDtypeStruct((M, N), BF16),
        scratch_shapes=[pltpu.VMEM((tm, K), BF16)],
        compiler_params=_params(("parallel", "arbitrary")),
        name=name,
    )(x, g, w)


def _out_proj_kernel(om_ref, omem_ref, w_ref, x_ref, o_ref):
    k_main = om_ref.shape[1]
    acc = _dot(om_ref[...], w_ref[:k_main, :]) + _dot(omem_ref[...], w_ref[k_main:, :])
    o_ref[...] = x_ref[...] + acc


def out_proj(o_main, o_mem, w, layer, x, *, name):
    M, k_main = o_main.shape
    k_mem = o_mem.shape[1]
    N = w.shape[-1]
    tm, tn = PROJ_TM, PROJ_TN
    return pl.pallas_call(
        _out_proj_kernel,
        grid=(M // tm, N // tn),
        in_specs=[
            pl.BlockSpec((tm, k_main), lambda i, j: (i, 0)),
            pl.BlockSpec((tm, k_mem), lambda i, j: (i, 0)),
            pl.BlockSpec((None, k_main + k_mem, tn), lambda i, j: (layer, 0, j)),
            pl.BlockSpec((tm, tn), lambda i, j: (i, j)),
        ],
        out_specs=pl.BlockSpec((tm, tn), lambda i, j: (i, j)),
        out_shape=jax.ShapeDtypeStruct((M, N), F32),
        compiler_params=_params(("parallel", "parallel")),
        name=name,
    )(o_main, o_mem, w, x)


def _rmsnorm_kernel(x_ref, g_ref, o_ref):
    def body(c, carry):
        r = pl.multiple_of(c * NORM_ROW_CHUNK, NORM_ROW_CHUNK)
        xc = x_ref[pl.ds(r, NORM_ROW_CHUNK), :]
        ms = jnp.mean(xc * xc, axis=-1, keepdims=True)
        o_ref[pl.ds(r, NORM_ROW_CHUNK), :] = xc * lax.rsqrt(ms + EPS) * g_ref[...]
        return carry

    lax.fori_loop(0, x_ref.shape[0] // NORM_ROW_CHUNK, body, 0)


def rmsnorm_rows(x, g):
    M, K = x.shape
    tm = FINAL_TM
    return pl.pallas_call(
        _rmsnorm_kernel,
        grid=(M // tm,),
        in_specs=[pl.BlockSpec((tm, K), lambda i: (i, 0)), pl.BlockSpec((1, K), lambda i: (0, 0))],
        out_specs=pl.BlockSpec((tm, K), lambda i: (i, 0)),
        out_shape=jax.ShapeDtypeStruct((M, K), F32),
        compiler_params=_params(("parallel",)),
        name="final_rmsnorm",
    )(x, g)


def _retention_kernel(g_ref, q_ref, k_ref, v_ref, gate_ref, cos_ref, sin_ref, decay_ref, xi_ref, zeta_ref,
                      o_ref, state_ref):
    @pl.when(pl.program_id(1) == 0)
    def _reset():
        state_ref[...] = jnp.zeros_like(state_ref)

    def rotate(t, cos, sin):
        return t * cos + pltpu.roll(t, RET_DK // 2, axis=1) * sin

    def chunk_body(c, carry):
        rows = pl.ds(pl.multiple_of(c * RET_CHUNK, RET_CHUNK), RET_CHUNK)
        cos = cos_ref[rows, :]
        sin = sin_ref[rows, :]
        for h in range(RET_HEADS):
            qk_cols = slice(h * RET_DK, (h + 1) * RET_DK)
            v_cols = slice(h * RET_DV, (h + 1) * RET_DV)
            q = rotate(q_ref[rows, qk_cols].astype(F32), cos, sin)
            k = rotate(k_ref[rows, qk_cols].astype(F32), cos, sin) * (RET_DK ** -0.5)
            v = v_ref[rows, v_cols]
            qb = q.astype(BF16)
            s = _dot_nt(qb, k.astype(BF16)) * decay_ref[h]
            state = state_ref[h]
            o = _dot(s.astype(BF16), v) + _dot(qb, state.astype(BF16)) * xi_ref[h]
            kz_t = (k * zeta_ref[h]).T.astype(BF16)
            state_ref[h] = g_ref[h] * state + _dot(kz_t, v)
            o = o * lax.rsqrt(jnp.mean(o * o, axis=-1, keepdims=True) + EPS)
            o_ref[rows, v_cols] = (o * _silu(gate_ref[rows, v_cols].astype(F32))).astype(o_ref.dtype)
        return carry

    lax.fori_loop(0, q_ref.shape[0] // RET_CHUNK, chunk_body, 0)


def _rotary_tables(S):
    half = RET_DK // 2
    inv = 1.0 / (RET_THETA_BASE ** jnp.linspace(0.0, 1.0, half, dtype=F32))
    ang = jnp.arange(S, dtype=F32)[:, None] * inv[None, :]
    cos, sin = jnp.cos(ang), jnp.sin(ang)
    return jnp.concatenate([cos, cos], axis=-1), jnp.concatenate([-sin, sin], axis=-1)


def _decay_tables():
    H, C = RET_HEADS, RET_CHUNK
    log_g = jnp.log1p(-(2.0 ** (-5.0 - jnp.arange(H, dtype=F32))))
    pos = jnp.arange(C, dtype=F32)
    rel = pos[:, None] - pos[None, :]
    decay_in = jnp.where(rel >= 0, jnp.exp(jnp.maximum(rel, 0.0)[None] * log_g[:, None, None]), 0.0)
    xi = jnp.exp((pos[None, :] + 1.0) * log_g[:, None])
    zeta = jnp.exp((C - 1.0 - pos[None, :]) * log_g[:, None])
    g_chunk = jnp.exp(C * log_g)
    return g_chunk, decay_in, xi[:, :, None], zeta[:, :, None]


def retention_gated(proj, B, S):
    M = B * S
    tq = RET_TQ
    spb = S // tq
    g_chunk, decay_in, xi, zeta = _decay_tables()
    cos, sin = _rotary_tables(S)
    k_blk = 1
    v_blk = 2 * RET_QK_W // RET_V_W
    gate_blk = v_blk + 1
    row = lambda b, i: b * spb + i
    const3 = lambda b, i: (0, 0, 0)
    return pl.pallas_call(
        _retention_kernel,
        grid=(B, spb),
        in_specs=[
            pl.BlockSpec(memory_space=pltpu.SMEM),
            pl.BlockSpec((tq, RET_QK_W), lambda b, i: (row(b, i), 0)),
            pl.BlockSpec((tq, RET_QK_W), lambda b, i: (row(b, i), k_blk)),
            pl.BlockSpec((tq, RET_V_W), lambda b, i: (row(b, i), v_blk)),
            pl.BlockSpec((tq, RET_V_W), lambda b, i: (row(b, i), gate_blk)),
            pl.BlockSpec((tq, RET_DK), lambda b, i: (i, 0)),
            pl.BlockSpec((tq, RET_DK), lambda b, i: (i, 0)),
            pl.BlockSpec((RET_HEADS, RET_CHUNK, RET_CHUNK), const3),
            pl.BlockSpec((RET_HEADS, RET_CHUNK, 1), const3),
            pl.BlockSpec((RET_HEADS, RET_CHUNK, 1), const3),
        ],
        out_specs=pl.BlockSpec((tq, RET_V_W), lambda b, i: (row(b, i), 0)),
        out_shape=jax.ShapeDtypeStruct((M, RET_V_W), BF16),
        scratch_shapes=[pltpu.VMEM((RET_HEADS, RET_DK, RET_DV), F32)],
        compiler_params=_params(("parallel", "arbitrary")),
        name="retention",
    )(g_chunk, proj, proj, proj, proj, cos, sin, decay_in, xi, zeta)


def _mem_attn_kernel(q_ref, gate_ref, k_ref, v_ref, o_ref):
    for h in range(MEM_HEADS):
        cols = slice(h * MEM_DH, (h + 1) * MEM_DH)
        s = _dot_nt(q_ref[:, cols], k_ref[:, cols]) * (MEM_DH ** -0.5)
        p = jnp.exp(s - jnp.max(s, axis=-1, keepdims=True))
        l = jnp.sum(p, axis=-1, keepdims=True)
        o = _dot(p.astype(BF16), v_ref[:, cols]) * (1.0 / l)
        o_ref[:, cols] = (o * _silu(gate_ref[:, cols].astype(F32))).astype(o_ref.dtype)


def mem_attention_gated(proj, mem_kv, B, S, q_blk):
    M = B * S
    tq = MEM_TQ
    spb = S // tq
    row = lambda b, i: b * spb + i
    return pl.pallas_call(
        _mem_attn_kernel,
        grid=(B, spb),
        in_specs=[
            pl.BlockSpec((tq, MEM_W), lambda b, i: (row(b, i), q_blk)),
            pl.BlockSpec((tq, MEM_W), lambda b, i: (row(b, i), q_blk + 1)),
            pl.BlockSpec((MEM_LEN, MEM_W), lambda b, i: (b, 0)),
            pl.BlockSpec((MEM_LEN, MEM_W), lambda b, i: (b, 1)),
        ],
        out_specs=pl.BlockSpec((tq, MEM_W), lambda b, i: (row(b, i), 0)),
        out_shape=jax.ShapeDtypeStruct((M, MEM_W), BF16),
        compiler_params=_params(("parallel", "parallel")),
        name="mem_attention",
    )(proj, proj, mem_kv, mem_kv)


DIL_TOK = max(d for _, d in DIL_CONFIG) * DIL_BLOCK
DIL_KEYS = 2 * DIL_BLOCK


def _residue_rows(dil):
    per_class = DIL_TOK // dil
    return per_class, DIL_BLOCK + per_class


def _deinterleave(src_ref, dst_ref, stage_ref, dil, halo):
    per_class = DIL_TOK // dil
    seg = halo + per_class
    if dil == 1:
        dst_ref[halo:, :] = src_ref[...]
        return
    stage_ref[...] = src_ref[...].astype(F32)
    for r in range(dil):
        dst_ref[r * seg + halo:(r + 1) * seg, :] = stage_ref[pl.ds(r, per_class, stride=dil), :].astype(BF16)


def _dilated_kernel(q0_ref, q1_ref, q2_ref, gate_ref, k0_ref, v0_ref, k1_ref, v1_ref, k2_ref, v2_ref, o_ref,
                    qd1_ref, qd2_ref, ke0_ref, ve0_ref, ke1_ref, ve1_ref, ke2_ref, ve2_ref,
                    stage_ref, acc_ref, m_ref, l_ref):
    j = pl.program_id(2)
    q_refs = (q0_ref, qd1_ref, qd2_ref)
    kv_refs = ((k0_ref, v0_ref, ke0_ref, ve0_ref), (k1_ref, v1_ref, ke1_ref, ve1_ref), (k2_ref, v2_ref, ke2_ref, ve2_ref))

    for (k_ref, v_ref, ke_ref, ve_ref), (_, dil) in zip(kv_refs, DIL_CONFIG):
        per_class, seg = _residue_rows(dil)

        @pl.when(j == 0)
        def _zero_halo(ke_ref=ke_ref, ve_ref=ve_ref, dil=dil, seg=seg):
            for r in range(dil):
                ke_ref[r * seg:r * seg + DIL_BLOCK, :] = jnp.zeros((DIL_BLOCK, DIL_DH), BF16)
                ve_ref[r * seg:r * seg + DIL_BLOCK, :] = jnp.zeros((DIL_BLOCK, DIL_DH), BF16)

        @pl.when(j > 0)
        def _carry_halo(ke_ref=ke_ref, ve_ref=ve_ref, dil=dil, seg=seg, per_class=per_class):
            for r in range(dil):
                ke_ref[r * seg:r * seg + DIL_BLOCK, :] = ke_ref[r * seg + per_class:(r + 1) * seg, :]
                ve_ref[r * seg:r * seg + DIL_BLOCK, :] = ve_ref[r * seg + per_class:(r + 1) * seg, :]

        _deinterleave(k_ref, ke_ref, stage_ref, dil, DIL_BLOCK)
        _deinterleave(v_ref, ve_ref, stage_ref, dil, DIL_BLOCK)
    _deinterleave(q1_ref, qd1_ref, stage_ref, DIL_CONFIG[1][1], 0)
    _deinterleave(q2_ref, qd2_ref, stage_ref, DIL_CONFIG[2][1], 0)

    qi = lax.broadcasted_iota(jnp.int32, (DIL_BLOCK, DIL_KEYS), 0)
    kk = lax.broadcasted_iota(jnp.int32, (DIL_BLOCK, DIL_KEYS), 1)
    band = (kk >= qi) & (kk <= qi + DIL_BLOCK)
    band_first = band & (kk >= jnp.where(j == 0, DIL_BLOCK, 0))
    scale = DIL_DH ** -0.5

    def attend(q, ke_ref, ve_ref, key0, mask, tok, first_group):
        kc = ke_ref[pl.ds(key0, DIL_KEYS), :]
        vc = ve_ref[pl.ds(key0, DIL_KEYS), :]
        s = jnp.where(mask, _dot_nt(q, kc) * scale, -1e30)
        m_blk = jnp.max(s, axis=-1, keepdims=True)
        if first_group:
            m_new = jnp.broadcast_to(m_blk, (DIL_BLOCK, DIL_DH))
        else:
            m_old = m_ref[tok, :]
            m_new = jnp.maximum(m_old, m_blk)
            alpha = jnp.exp(m_old - m_new)
        p = jnp.exp(s - jnp.concatenate([m_new, m_new], axis=1))
        l_blk = jnp.sum(p, axis=-1, keepdims=True)
        pv = _dot(p.astype(BF16), vc)
        if first_group:
            l_ref[tok, :] = jnp.broadcast_to(l_blk, (DIL_BLOCK, DIL_DH))
            acc_ref[tok, :] = pv
        else:
            l_ref[tok, :] = alpha * l_ref[tok, :] + l_blk
            acc_ref[tok, :] = alpha * acc_ref[tok, :] + pv
        m_ref[tok, :] = m_new

    def dense_block(blk, mask):
        row0 = blk * DIL_BLOCK
        if not isinstance(blk, int):
            row0 = pl.multiple_of(row0, DIL_BLOCK)
        rows = pl.ds(row0, DIL_BLOCK)
        attend(q0_ref[rows, :], ke0_ref, ve0_ref, row0, mask, rows, True)

    dense_block(0, band_first)

    def dense_body(blk, carry):
        dense_block(blk, band)
        return carry

    lax.fori_loop(1, DIL_TOK // DIL_BLOCK, dense_body, 0)

    for group in range(1, len(DIL_CONFIG)):
        dil = DIL_CONFIG[group][1]
        per_class, seg = _residue_rows(dil)
        _, _, ke_ref, ve_ref = kv_refs[group]
        for r in range(dil):
            for blk in range(per_class // DIL_BLOCK):
                q = q_refs[group][r * per_class + blk * DIL_BLOCK:r * per_class + (blk + 1) * DIL_BLOCK, :]
                tok = pl.ds(blk * DIL_BLOCK * dil + r, DIL_BLOCK, stride=dil)
                attend(q, ke_ref, ve_ref, r * seg + blk * DIL_BLOCK, band_first if blk == 0 else band, tok, False)

    def finish(c, carry):
        rows = pl.ds(pl.multiple_of(c * DIL_BLOCK, DIL_BLOCK), DIL_BLOCK)
        o = acc_ref[rows, :] * (1.0 / l_ref[rows, :]) * _silu(gate_ref[rows, :].astype(F32))
        o_ref[rows, :] = o.astype(o_ref.dtype)
        return carry

    lax.fori_loop(0, DIL_TOK // DIL_BLOCK, finish, 0)


def dilated_gated(proj, kv, B, S):
    assert S % DIL_TOK == 0
    nblk = S // DIL_TOK
    heads_per_group = DIL_W // DIL_DH

    def head_cols(group_col):
        return pl.BlockSpec((DIL_TOK, DIL_DH), lambda b, h, j: (b * nblk + j, group_col * heads_per_group + h))

    ext = lambda dil: pltpu.VMEM((dil * _residue_rows(dil)[1], DIL_DH), BF16)
    tok_f32 = pltpu.VMEM((DIL_TOK, DIL_DH), F32)
    return pl.pallas_call(
        _dilated_kernel,
        grid=(B, DIL_HEADS, nblk),
        in_specs=[head_cols(c) for c in range(4)] + [head_cols(c) for c in range(6)],
        out_specs=pl.BlockSpec((DIL_TOK, DIL_DH), lambda b, h, j: (b * nblk + j, h)),
        out_shape=jax.ShapeDtypeStruct((B * S, DIL_W), BF16),
        scratch_shapes=[pltpu.VMEM((DIL_TOK, DIL_DH), BF16), pltpu.VMEM((DIL_TOK, DIL_DH), BF16)]
        + [ext(d) for _, d in DIL_CONFIG for _ in range(2)]
        + [tok_f32, tok_f32, tok_f32, tok_f32],
        compiler_params=_params(("arbitrary", "arbitrary", "arbitrary")),
        name="dilated_attention",
    )(proj, proj, proj, proj, kv, kv, kv, kv, kv, kv)


def kernel(x, mem, norm_a, w_in_a, w_out_a, norm_b, w_in_b, w_out_b, w_mem_kv, mem_norm_g, kv_norm_g, w_kv, final_norm_g):
    B, S, D = x.shape
    xr = x.reshape(B * S, D)
    mem_r = mem.reshape(B * mem.shape[1], D)

    w_in_a, w_out_a, w_in_b, w_out_b, w_mem_kv = (t.astype(BF16) for t in (w_in_a, w_out_a, w_in_b, w_out_b, w_mem_kv))
    w_kv = w_kv.astype(BF16)[None]
    norm_a = norm_a[:, None, :]
    norm_b = norm_b[:, None, :]
    mem_g = mem_norm_g[None, None, :]
    kv_g = kv_norm_g[None, None, :]

    kv = None
    for layer in range(DEPTH):
        mem_kv = norm_matmul(mem_r, mem_g, 0, w_mem_kv, layer, name=f"mem_kv_proj{layer}")
        if layer < N_A:
            proj = norm_matmul(xr, norm_a, layer, w_in_a, layer, name=f"in_proj_a{layer}")
            o_main = retention_gated(proj, B, S)
            o_mem = mem_attention_gated(proj, mem_kv, B, S, (IN_W_A - 2 * MEM_W) // MEM_W)
            xr = out_proj(o_main, o_mem, w_out_a, layer, xr, name=f"out_proj_a{layer}")
            if layer == N_A - 1:
                kv = norm_matmul(xr, kv_g, 0, w_kv, 0, name="shared_kv_proj")
        else:
            jb = layer - N_A
            proj = norm_matmul(xr, norm_b, jb, w_in_b, jb, name=f"in_proj_b{jb}")
            o_main = dilated_gated(proj, kv, B, S)
            o_mem = mem_attention_gated(proj, mem_kv, B, S, (IN_W_B - 2 * MEM_W) // MEM_W)
            xr = out_proj(o_main, o_mem, w_out_b, jb, xr, name=f"out_proj_b{jb}")
    return rmsnorm_rows(xr, final_norm_g[None, :]).reshape(B, S, D)
```

```python
import functools
import math

import jax
import jax.numpy as jnp
from jax import lax
from jax.experimental import pallas as pl
from jax.experimental.pallas import tpu as pltpu

F32 = jnp.float32
BF16 = jnp.bfloat16

D_MODEL = 2048
DEPTH = 4
N_A = DEPTH // 2
MEM_LEN = 256
RET_HEADS = 8
RET_DK = D_MODEL // 16
RET_DV = 2 * RET_DK
RET_CHUNK = 128
RET_THETA_BASE = 10000.0
DIL_CONFIG = ((128, 1), (512, 4), (2048, 16))
DIL_HEADS = 16
DIL_DH = D_MODEL // 16
DIL_BLOCK = 128
MEM_HEADS = 4
MEM_DH = D_MODEL // 8
EPS = 1e-6

RET_QK_W = RET_HEADS * RET_DK
RET_V_W = RET_HEADS * RET_DV
DIL_W = DIL_HEADS * DIL_DH
MEM_W = MEM_HEADS * MEM_DH
IN_W_A = 2 * RET_QK_W + 2 * RET_V_W + 2 * MEM_W
IN_W_B = len(DIL_CONFIG) * DIL_W + DIL_W + 2 * MEM_W
KV_W = 2 * len(DIL_CONFIG) * DIL_W

LANES = 128
VMEM_LIMIT_BYTES = 56 * 2**20
NORM_ROW_CHUNK = 64
PROJ_TM, PROJ_TN = 1024, 1024
RET_TQ = 512
MEM_TQ = 512
FINAL_TM = 512


def _params(semantics):
    return pltpu.CompilerParams(dimension_semantics=semantics, vmem_limit_bytes=VMEM_LIMIT_BYTES)


def _dot(a, b):
    return jnp.dot(a, b, preferred_element_type=F32)


def _dot_nt(a, b):
    return lax.dot_general(a, b, (((1,), (1,)), ((), ())), preferred_element_type=F32)


def _silu(x):
    return x * (1.0 / (1.0 + jnp.exp(-x)))


def _norm_matmul_kernel(x_ref, g_ref, w_ref, o_ref, xn_ref):
    @pl.when(pl.program_id(1) == 0)
    def _normalize():
        def body(c, carry):
            r = pl.multiple_of(c * NORM_ROW_CHUNK, NORM_ROW_CHUNK)
            xc = x_ref[pl.ds(r, NORM_ROW_CHUNK), :]
            ms = jnp.mean(xc * xc, axis=-1, keepdims=True)
            xn_ref[pl.ds(r, NORM_ROW_CHUNK), :] = (xc * lax.rsqrt(ms + EPS) * g_ref[...]).astype(BF16)
            return carry

        lax.fori_loop(0, x_ref.shape[0] // NORM_ROW_CHUNK, body, 0)

    o_ref[...] = _dot(xn_ref[...], w_ref[...]).astype(o_ref.dtype)


def norm_matmul(x, g, g_idx, w, w_idx, *, name):
    M, K = x.shape
    N = w.shape[-1]
    tm, tn = min(PROJ_TM, M), min(PROJ_TN, N)
    return pl.pallas_call(
        _norm_matmul_kernel,
        grid=(M // tm, N // tn),
        in_specs=[
            pl.BlockSpec((tm, K), lambda i, j: (i, 0)),
            pl.BlockSpec((None, 1, K), lambda i, j: (g_idx, 0, 0)),
            pl.BlockSpec((None, K, tn), lambda i, j: (w_idx, 0, j)),
        ],
        out_specs=pl.BlockSpec((tm, tn), lambda i, j: (i, j)),
        out_shape=jax.ShapeDtypeStruct((M, N), BF16),
        scratch_shapes=[pltpu.VMEM((tm, K), BF16)],
        compiler_params=_params(("parallel", "arbitrary")),
        name=name,
    )(x, g, w)


def _out_proj_kernel(om_ref, omem_ref, w_ref, x_ref, o_ref):
    k_main = om_ref.shape[1]
    acc = _dot(om_ref[...], w_ref[:k_main, :]) + _dot(omem_ref[...], w_ref[k_main:, :])
    o_ref[...] = x_ref[...] + acc


def out_proj(o_main, o_mem, w, layer, x, *, name):
    M, k_main = o_main.shape
    k_mem = o_mem.shape[1]
    N = w.shape[-1]
    tm, tn = PROJ_TM, PROJ_TN
    return pl.pallas_call(
        _out_proj_kernel,
        grid=(M // tm, N // tn),
        in_specs=[
            pl.BlockSpec((tm, k_main), lambda i, j: (i, 0)),
            pl.BlockSpec((tm, k_mem), lambda i, j: (i, 0)),
            pl.BlockSpec((None, k_main + k_mem, tn), lambda i, j: (layer, 0, j)),
            pl.BlockSpec((tm, tn), lambda i, j: (i, j)),
        ],
        out_specs=pl.BlockSpec((tm, tn), lambda i, j: (i, j)),
        out_shape=jax.ShapeDtypeStruct((M, N), F32),
        compiler_params=_params(("parallel", "parallel")),
        name=name,
    )(o_main, o_mem, w, x)


def _rmsnorm_kernel(x_ref, g_ref, o_ref):
    def body(c, carry):
        r = pl.multiple_of(c * NORM_ROW_CHUNK, NORM_ROW_CHUNK)
        xc = x_ref[pl.ds(r, NORM_ROW_CHUNK), :]
        ms = jnp.mean(xc * xc, axis=-1, keepdims=True)
        o_ref[pl.ds(r, NORM_ROW_CHUNK), :] = xc * lax.rsqrt(ms + EPS) * g_ref[...]
        return carry

    lax.fori_loop(0, x_ref.shape[0] // NORM_ROW_CHUNK, body, 0)


def rmsnorm_rows(x, g):
    M, K = x.shape
    tm = FINAL_TM
    return pl.pallas_call(
        _rmsnorm_kernel,
        grid=(M // tm,),
        in_specs=[pl.BlockSpec((tm, K), lambda i: (i, 0)), pl.BlockSpec((1, K), lambda i: (0, 0))],
        out_specs=pl.BlockSpec((tm, K), lambda i: (i, 0)),
        out_shape=jax.ShapeDtypeStruct((M, K), F32),
        compiler_params=_params(("parallel",)),
        name="final_rmsnorm",
    )(x, g)


def _retention_kernel(g_ref, q_ref, k_ref, v_ref, gate_ref, cos_ref, sin_ref, decay_ref, xi_ref, zeta_ref,
                      o_ref, state_ref):
    @pl.when(pl.program_id(1) == 0)
    def _reset():
        state_ref[...] = jnp.zeros_like(state_ref)

    def rotate(t, cos, sin):
        return t * cos + pltpu.roll(t, RET_DK // 2, axis=1) * sin

    def chunk_body(c, carry):
        rows = pl.ds(pl.multiple_of(c * RET_CHUNK, RET_CHUNK), RET_CHUNK)
        cos = cos_ref[rows, :]
        sin = sin_ref[rows, :]
        for h in range(RET_HEADS):
            qk_cols = slice(h * RET_DK, (h + 1) * RET_DK)
            v_cols = slice(h * RET_DV, (h + 1) * RET_DV)
            q = rotate(q_ref[rows, qk_cols].astype(F32), cos, sin)
            k = rotate(k_ref[rows, qk_cols].astype(F32), cos, sin) * (RET_DK ** -0.5)
            v = v_ref[rows, v_cols]
            qb = q.astype(BF16)
            s = _dot_nt(qb, k.astype(BF16)) * decay_ref[h]
            state = state_ref[h]
            o = _dot(s.astype(BF16), v) + _dot(qb, state.astype(BF16)) * xi_ref[h]
            kz_t = (k * zeta_ref[h]).T.astype(BF16)
            state_ref[h] = g_ref[h] * state + _dot(kz_t, v)
            o = o * lax.rsqrt(jnp.mean(o * o, axis=-1, keepdims=True) + EPS)
            o_ref[rows, v_cols] = (o * _silu(gate_ref[rows, v_cols].astype(F32))).astype(o_ref.dtype)
        return carry

    lax.fori_loop(0, q_ref.shape[0] // RET_CHUNK, chunk_body, 0)


def _rotary_tables(S):
    half = RET_DK // 2
    inv = 1.0 / (RET_THETA_BASE ** jnp.linspace(0.0, 1.0, half, dtype=F32))
    ang = jnp.arange(S, dtype=F32)[:, None] * inv[None, :]
    cos, sin = jnp.cos(ang), jnp.sin(ang)
    return jnp.concatenate([cos, cos], axis=-1), jnp.concatenate([-sin, sin], axis=-1)


def _decay_tables():
    H, C = RET_HEADS, RET_CHUNK
    log_g = jnp.log1p(-(2.0 ** (-5.0 - jnp.arange(H, dtype=F32))))
    pos = jnp.arange(C, dtype=F32)
    rel = pos[:, None] - pos[None, :]
    decay_in = jnp.where(rel >= 0, jnp.exp(jnp.maximum(rel, 0.0)[None] * log_g[:, None, None]), 0.0)
    xi = jnp.exp((pos[None, :] + 1.0) * log_g[:, None])
    zeta = jnp.exp((C - 1.0 - pos[None, :]) * log_g[:, None])
    g_chunk = jnp.exp(C * log_g)
    return g_chunk, decay_in, xi[:, :, None], zeta[:, :, None]


def retention_gated(proj, B, S):
    M = B * S
    tq = RET_TQ
    spb = S // tq
    g_chunk, decay_in, xi, zeta = _decay_tables()
    cos, sin = _rotary_tables(S)
    k_blk = 1
    v_blk = 2 * RET_QK_W // RET_V_W
    gate_blk = v_blk + 1
    row = lambda b, i: b * spb + i
    const3 = lambda b, i: (0, 0, 0)
    return pl.pallas_call(
        _retention_kernel,
        grid=(B, spb),
        in_specs=[
            pl.BlockSpec(memory_space=pltpu.SMEM),
            pl.BlockSpec((tq, RET_QK_W), lambda b, i: (row(b, i), 0)),
            pl.BlockSpec((tq, RET_QK_W), lambda b, i: (row(b, i), k_blk)),
            pl.BlockSpec((tq, RET_V_W), lambda b, i: (row(b, i), v_blk)),
            pl.BlockSpec((tq, RET_V_W), lambda b, i: (row(b, i), gate_blk)),
            pl.BlockSpec((tq, RET_DK), lambda b, i: (i, 0)),
            pl.BlockSpec((tq, RET_DK), lambda b, i: (i, 0)),
            pl.BlockSpec((RET_HEADS, RET_CHUNK, RET_CHUNK), const3),
            pl.BlockSpec((RET_HEADS, RET_CHUNK, 1), const3),
            pl.BlockSpec((RET_HEADS, RET_CHUNK, 1), const3),
        ],
        out_specs=pl.BlockSpec((tq, RET_V_W), lambda b, i: (row(b, i), 0)),
        out_shape=jax.ShapeDtypeStruct((M, RET_V_W), BF16),
        scratch_shapes=[pltpu.VMEM((RET_HEADS, RET_DK, RET_DV), F32)],
        compiler_params=_params(("parallel", "arbitrary")),
        name="retention",
    )(g_chunk, proj, proj, proj, proj, cos, sin, decay_in, xi, zeta)


def _mem_attn_kernel(q_ref, gate_ref, k_ref, v_ref, o_ref):
    for h in range(MEM_HEADS):
        cols = slice(h * MEM_DH, (h + 1) * MEM_DH)
        s = _dot_nt(q_ref[:, cols], k_ref[:, cols]) * (MEM_DH ** -0.5)
        p = jnp.exp(s - jnp.max(s, axis=-1, keepdims=True))
        l = jnp.sum(p, axis=-1, keepdims=True)
        o = _dot(p.astype(BF16), v_ref[:, cols]) * (1.0 / l)
        o_ref[:, cols] = (o * _silu(gate_ref[:, cols].astype(F32))).astype(o_ref.dtype)


def mem_attention_gated(proj, mem_kv, B, S, q_blk):
    M = B * S
    tq = MEM_TQ
    spb = S // tq
    row = lambda b, i: b * spb + i
    return pl.pallas_call(
        _mem_attn_kernel,
        grid=(B, spb),
        in_specs=[
            pl.BlockSpec((tq, MEM_W), lambda b, i: (row(b, i), q_blk)),
            pl.BlockSpec((tq, MEM_W), lambda b, i: (row(b, i), q_blk + 1)),
            pl.BlockSpec((MEM_LEN, MEM_W), lambda b, i: (b, 0)),
            pl.BlockSpec((MEM_LEN, MEM_W), lambda b, i: (b, 1)),
        ],
        out_specs=pl.BlockSpec((tq, MEM_W), lambda b, i: (row(b, i), 0)),
        out_shape=jax.ShapeDtypeStruct((M, MEM_W), BF16),
        compiler_params=_params(("parallel", "parallel")),
        name="mem_attention",
    )(proj, proj, mem_kv, mem_kv)


DIL_TOK = max(d for _, d in DIL_CONFIG) * DIL_BLOCK
DIL_KEYS = 2 * DIL_BLOCK


def _residue_rows(dil):
    per_class = DIL_TOK // dil
    return per_class, DIL_BLOCK + per_class


def _deinterleave(src_ref, dst_ref, stage_ref, dil, halo):
    per_class = DIL_TOK // dil
    seg = halo + per_class
    if dil == 1:
        dst_ref[halo:, :] = src_ref[...]
        return
    stage_ref[...] = src_ref[...].astype(F32)
    for r in range(dil):
        dst_ref[r * seg + halo:(r + 1) * seg, :] = stage_ref[pl.ds(r, per_class, stride=dil), :].astype(BF16)


def _dilated_kernel(q0_ref, q1_ref, q2_ref, gate_ref, k0_ref, v0_ref, k1_ref, v1_ref, k2_ref, v2_ref, o_ref,
                    qd1_ref, qd2_ref, ke0_ref, ve0_ref, ke1_ref, ve1_ref, ke2_ref, ve2_ref,
                    stage_ref, acc_ref, m_ref, l_ref):
    j = pl.program_id(2)
    q_refs = (q0_ref, qd1_ref, qd2_ref)
    kv_refs = ((k0_ref, v0_ref, ke0_ref, ve0_ref), (k1_ref, v1_ref, ke1_ref, ve1_ref), (k2_ref, v2_ref, ke2_ref, ve2_ref))

    for (k_ref, v_ref, ke_ref, ve_ref), (_, dil) in zip(kv_refs, DIL_CONFIG):
        per_class, seg = _residue_rows(dil)

        @pl.when(j == 0)
        def _zero_halo(ke_ref=ke_ref, ve_ref=ve_ref, dil=dil, seg=seg):
            for r in range(dil):
                ke_ref[r * seg:r * seg + DIL_BLOCK, :] = jnp.zeros((DIL_BLOCK, DIL_DH), BF16)
                ve_ref[r * seg:r * seg + DIL_BLOCK, :] = jnp.zeros((DIL_BLOCK, DIL_DH), BF16)

        @pl.when(j > 0)
        def _carry_halo(ke_ref=ke_ref, ve_ref=ve_ref, dil=dil, seg=seg, per_class=per_class):
            for r in range(dil):
                ke_ref[r * seg:r * seg + DIL_BLOCK, :] = ke_ref[r * seg + per_class:(r + 1) * seg, :]
                ve_ref[r * seg:r * seg + DIL_BLOCK, :] = ve_ref[r * seg + per_class:(r + 1) * seg, :]

        _deinterleave(k_ref, ke_ref, stage_ref, dil, DIL_BLOCK)
        _deinterleave(v_ref, ve_ref, stage_ref, dil, DIL_BLOCK)
    _deinterleave(q1_ref, qd1_ref, stage_ref, DIL_CONFIG[1][1], 0)
    _deinterleave(q2_ref, qd2_ref, stage_ref, DIL_CONFIG[2][1], 0)

    qi = lax.broadcasted_iota(jnp.int32, (DIL_BLOCK, DIL_KEYS), 0)
    kk = lax.broadcasted_iota(jnp.int32, (DIL_BLOCK, DIL_KEYS), 1)
    band = (kk >= qi) & (kk <= qi + DIL_BLOCK)
    band_first = band & (kk >= jnp.where(j == 0, DIL_BLOCK, 0))
    scale = DIL_DH ** -0.5

    def attend(q, ke_ref, ve_ref, key0, mask, tok, first_group):
        kc = ke_ref[pl.ds(key0, DIL_KEYS), :]
        vc = ve_ref[pl.ds(key0, DIL_KEYS), :]
        s = jnp.where(mask, _dot_nt(q, kc) * scale, -1e30)
        m_blk = jnp.max(s, axis=-1, keepdims=True)
        if first_group:
            m_new = jnp.broadcast_to(m_blk, (DIL_BLOCK, DIL_DH))
        else:
            m_old = m_ref[tok, :]
            m_new = jnp.maximum(m_old, m_blk)
            alpha = jnp.exp(m_old - m_new)
        p = jnp.exp(s - jnp.concatenate([m_new, m_new], axis=1))
        l_blk = jnp.sum(p, axis=-1, keepdims=True)
        pv = _dot(p.astype(BF16), vc)
        if first_group:
            l_ref[tok, :] = jnp.broadcast_to(l_blk, (DIL_BLOCK, DIL_DH))
            acc_ref[tok, :] = pv
        else:
            l_ref[tok, :] = alpha * l_ref[tok, :] + l_blk
            acc_ref[tok, :] = alpha * acc_ref[tok, :] + pv
        m_ref[tok, :] = m_new

    for blk in range(DIL_TOK // DIL_BLOCK):
        rows = pl.ds(blk * DIL_BLOCK, DIL_BLOCK)
        attend(q0_ref[rows, :], ke0_ref, ve0_ref, blk * DIL_BLOCK, band_first if blk == 0 else band, rows, True)

    for group in range(1, len(DIL_CONFIG)):
        dil = DIL_CONFIG[group][1]
        per_class, seg = _residue_rows(dil)
        _, _, ke_ref, ve_ref = kv_refs[group]
        for r in range(dil):
            for blk in range(per_class // DIL_BLOCK):
                q = q_refs[group][r * per_class + blk * DIL_BLOCK:r * per_class + (blk + 1) * DIL_BLOCK, :]
                tok = pl.ds(blk * DIL_BLOCK * dil + r, DIL_BLOCK, stride=dil)
                attend(q, ke_ref, ve_ref, r * seg + blk * DIL_BLOCK, band_first if blk == 0 else band, tok, False)

    def finish(c, carry):
        rows = pl.ds(pl.multiple_of(c * DIL_BLOCK, DIL_BLOCK), DIL_BLOCK)
        o = acc_ref[rows, :] * (1.0 / l_ref[rows, :]) * _silu(gate_ref[rows, :].astype(F32))
        o_ref[rows, :] = o.astype(o_ref.dtype)
        return carry

    lax.fori_loop(0, DIL_TOK // DIL_BLOCK, finish, 0)


def dilated_gated(proj, kv, B, S):
    assert S % DIL_TOK == 0
    nblk = S // DIL_TOK
    heads_per_group = DIL_W // DIL_DH

    def head_cols(group_col):
        return pl.BlockSpec((DIL_TOK, DIL_DH), lambda b, h, j: (b * nblk + j, group_col * heads_per_group + h))

    ext = lambda dil: pltpu.VMEM((dil * _residue_rows(dil)[1], DIL_DH), BF16)
    tok_f32 = pltpu.VMEM((DIL_TOK, DIL_DH), F32)
    return pl.pallas_call(
        _dilated_kernel,
        grid=(B, DIL_HEADS, nblk),
        in_specs=[head_cols(c) for c in range(4)] + [head_cols(c) for c in range(6)],
        out_specs=pl.BlockSpec((DIL_TOK, DIL_DH), lambda b, h, j: (b * nblk + j, h)),
        out_shape=jax.ShapeDtypeStruct((B * S, DIL_W), BF16),
        scratch_shapes=[pltpu.VMEM((DIL_TOK, DIL_DH), BF16), pltpu.VMEM((DIL_TOK, DIL_DH), BF16)]
        + [ext(d) for _, d in DIL_CONFIG for _ in range(2)]
        + [tok_f32, tok_f32, tok_f32, tok_f32],
        compiler_params=_params(("arbitrary", "arbitrary", "arbitrary")),
        name="dilated_attention",
    )(proj, proj, proj, proj, kv, kv, kv, kv, kv, kv)


def kernel(x, mem, norm_a, w_in_a, w_out_a, norm_b, w_in_b, w_out_b, w_mem_kv, mem_norm_g, kv_norm_g, w_kv, final_norm_g):
    B, S, D = x.shape
    xr = x.reshape(B * S, D)
    mem_r = mem.reshape(B * mem.shape[1], D)

    w_in_a, w_out_a, w_in_b, w_out_b, w_mem_kv = (t.astype(BF16) for t in (w_in_a, w_out_a, w_in_b, w_out_b, w_mem_kv))
    w_kv = w_kv.astype(BF16)[None]
    norm_a = norm_a[:, None, :]
    norm_b = norm_b[:, None, :]
    mem_g = mem_norm_g[None, None, :]
    kv_g = kv_norm_g[None, None, :]

    kv = None
    for layer in range(DEPTH):
        mem_kv = norm_matmul(mem_r, mem_g, 0, w_mem_kv, layer, name=f"mem_kv_proj{layer}")
        if layer < N_A:
            proj = norm_matmul(xr, norm_a, layer, w_in_a, layer, name=f"in_proj_a{layer}")
            o_main = retention_gated(proj, B, S)
            o_mem = mem_attention_gated(proj, mem_kv, B, S, (IN_W_A - 2 * MEM_W) // MEM_W)
            xr = out_proj(o_main, o_mem, w_out_a, layer, xr, name=f"out_proj_a{layer}")
            if layer == N_A - 1:
                kv = norm_matmul(xr, kv_g, 0, w_kv, 0, name="shared_kv_proj")
        else:
            jb = layer - N_A
            proj = norm_matmul(xr, norm_b, jb, w_in_b, jb, name=f"in_proj_b{jb}")
            o_main = dilated_gated(proj, kv, B, S)
            o_mem = mem_attention_gated(proj, mem_kv, B, S, (IN_W_B - 2 * MEM_W) // MEM_W)
            xr = out_proj(o_main, o_mem, w_out_b, jb, xr, name=f"out_proj_b{jb}")
    return rmsnorm_rows(xr, final_norm_g[None, :]).reshape(B, S, D)
```

```python
import functools
import math

import jax
import jax.numpy as jnp
from jax import lax
from jax.experimental import pallas as pl
from jax.experimental.pallas import tpu as pltpu

F32 = jnp.float32
BF16 = jnp.bfloat16

D_MODEL = 2048
DEPTH = 4
N_A = DEPTH // 2
MEM_LEN = 256
RET_HEADS = 8
RET_DK = D_MODEL // 16
RET_DV = 2 * RET_DK
RET_CHUNK = 128
RET_THETA_BASE = 10000.0
DIL_CONFIG = ((128, 1), (512, 4), (2048, 16))
DIL_HEADS = 16
DIL_DH = D_MODEL // 16
DIL_BLOCK = 128
MEM_HEADS = 4
MEM_DH = D_MODEL // 8
EPS = 1e-6

RET_QK_W = RET_HEADS * RET_DK
RET_V_W = RET_HEADS * RET_DV
DIL_W = DIL_HEADS * DIL_DH
MEM_W = MEM_HEADS * MEM_DH
IN_W_A = 2 * RET_QK_W + 2 * RET_V_W + 2 * MEM_W
IN_W_B = len(DIL_CONFIG) * DIL_W + DIL_W + 2 * MEM_W
KV_W = 2 * len(DIL_CONFIG) * DIL_W

LANES = 128
VMEM_LIMIT_BYTES = 56 * 2**20
NORM_ROW_CHUNK = 64
PROJ_TM, PROJ_TN = 1024, 1024
PERM_TOK = 256
RET_TQ = 512
MEM_TQ = 512
FINAL_TM = 512


def _params(semantics):
    return pltpu.CompilerParams(dimension_semantics=semantics, vmem_limit_bytes=VMEM_LIMIT_BYTES)


def _dot(a, b):
    return jnp.dot(a, b, preferred_element_type=F32)


def _dot_nt(a, b):
    return lax.dot_general(a, b, (((1,), (1,)), ((), ())), preferred_element_type=F32)


def _silu(x):
    return x * (1.0 / (1.0 + jnp.exp(-x)))


def _residue_order_matrix(dil):
    per_class = PERM_TOK // dil
    dst = lax.broadcasted_iota(jnp.int32, (PERM_TOK, PERM_TOK), 0)
    src = lax.broadcasted_iota(jnp.int32, (PERM_TOK, PERM_TOK), 1)
    wanted = (dst & (per_class - 1)) * dil + (dst >> int(math.log2(per_class)))
    return jnp.where(src == wanted, 1.0, 0.0).astype(BF16)


def _norm_matmul_kernel(x_ref, g_ref, w_ref, o_ref, xn_ref, *, dils, tile_copy):
    j = pl.program_id(1)

    @pl.when(j == 0)
    def _normalize():
        def body(c, carry):
            r = pl.multiple_of(c * NORM_ROW_CHUNK, NORM_ROW_CHUNK)
            xc = x_ref[pl.ds(r, NORM_ROW_CHUNK), :]
            ms = jnp.mean(xc * xc, axis=-1, keepdims=True)
            xn_ref[0, pl.ds(r, NORM_ROW_CHUNK), :] = (xc * lax.rsqrt(ms + EPS) * g_ref[...]).astype(BF16)
            return carry

        lax.fori_loop(0, x_ref.shape[0] // NORM_ROW_CHUNK, body, 0)
        for v in range(1, len(dils)):
            order = _residue_order_matrix(dils[v])
            for b in range(x_ref.shape[0] // PERM_TOK):
                rows = slice(b * PERM_TOK, (b + 1) * PERM_TOK)
                xn_ref[v, rows, :] = _dot(order, xn_ref[0, rows, :]).astype(BF16)

    copy = 0
    for t, v in enumerate(tile_copy):
        if v:
            copy = copy + v * (j == t).astype(jnp.int32)
    o_ref[...] = _dot(xn_ref[copy], w_ref[...]).astype(o_ref.dtype)


def norm_matmul(x, g, g_idx, w, w_idx, *, name, dils=(1,), tile_copy=None):
    M, K = x.shape
    N = w.shape[-1]
    tm, tn = min(PROJ_TM, M), min(PROJ_TN, N)
    tile_copy = tile_copy or (0,) * (N // tn)
    assert dils[0] == 1 and len(tile_copy) == N // tn and tm % PERM_TOK == 0
    return pl.pallas_call(
        functools.partial(_norm_matmul_kernel, dils=dils, tile_copy=tile_copy),
        grid=(M // tm, N // tn),
        in_specs=[
            pl.BlockSpec((tm, K), lambda i, j: (i, 0)),
            pl.BlockSpec((None, 1, K), lambda i, j: (g_idx, 0, 0)),
            pl.BlockSpec((None, K, tn), lambda i, j: (w_idx, 0, j)),
        ],
        out_specs=pl.BlockSpec((tm, tn), lambda i, j: (i, j)),
        out_shape=jax.ShapeDtypeStruct((M, N), BF16),
        scratch_shapes=[pltpu.VMEM((len(dils), tm, K), BF16)],
        compiler_params=_params(("parallel", "arbitrary")),
        name=name,
    )(x, g, w)


def _out_proj_kernel(om_ref, omem_ref, w_ref, x_ref, o_ref):
    k_main = om_ref.shape[1]
    acc = _dot(om_ref[...], w_ref[:k_main, :]) + _dot(omem_ref[...], w_ref[k_main:, :])
    o_ref[...] = x_ref[...] + acc


def out_proj(o_main, o_mem, w, layer, x, *, name):
    M, k_main = o_main.shape
    k_mem = o_mem.shape[1]
    N = w.shape[-1]
    tm, tn = PROJ_TM, PROJ_TN
    return pl.pallas_call(
        _out_proj_kernel,
        grid=(M // tm, N // tn),
        in_specs=[
            pl.BlockSpec((tm, k_main), lambda i, j: (i, 0)),
            pl.BlockSpec((tm, k_mem), lambda i, j: (i, 0)),
            pl.BlockSpec((None, k_main + k_mem, tn), lambda i, j: (layer, 0, j)),
            pl.BlockSpec((tm, tn), lambda i, j: (i, j)),
        ],
        out_specs=pl.BlockSpec((tm, tn), lambda i, j: (i, j)),
        out_shape=jax.ShapeDtypeStruct((M, N), F32),
        compiler_params=_params(("parallel", "parallel")),
        name=name,
    )(o_main, o_mem, w, x)


def _rmsnorm_kernel(x_ref, g_ref, o_ref):
    def body(c, carry):
        r = pl.multiple_of(c * NORM_ROW_CHUNK, NORM_ROW_CHUNK)
        xc = x_ref[pl.ds(r, NORM_ROW_CHUNK), :]
        ms = jnp.mean(xc * xc, axis=-1, keepdims=True)
        o_ref[pl.ds(r, NORM_ROW_CHUNK), :] = xc * lax.rsqrt(ms + EPS) * g_ref[...]
        return carry

    lax.fori_loop(0, x_ref.shape[0] // NORM_ROW_CHUNK, body, 0)


def rmsnorm_rows(x, g):
    M, K = x.shape
    tm = FINAL_TM
    return pl.pallas_call(
        _rmsnorm_kernel,
        grid=(M // tm,),
        in_specs=[pl.BlockSpec((tm, K), lambda i: (i, 0)), pl.BlockSpec((1, K), lambda i: (0, 0))],
        out_specs=pl.BlockSpec((tm, K), lambda i: (i, 0)),
        out_shape=jax.ShapeDtypeStruct((M, K), F32),
        compiler_params=_params(("parallel",)),
        name="final_rmsnorm",
    )(x, g)


def _retention_kernel(g_ref, q_ref, k_ref, v_ref, gate_ref, cos_ref, sin_ref, decay_ref, xi_ref, zeta_ref,
                      o_ref, state_ref):
    @pl.when(pl.program_id(1) == 0)
    def _reset():
        state_ref[...] = jnp.zeros_like(state_ref)

    def rotate(t, cos, sin):
        return t * cos + pltpu.roll(t, RET_DK // 2, axis=1) * sin

    def chunk_body(c, carry):
        rows = pl.ds(pl.multiple_of(c * RET_CHUNK, RET_CHUNK), RET_CHUNK)
        cos = cos_ref[rows, :]
        sin = sin_ref[rows, :]
        for h in range(RET_HEADS):
            qk_cols = slice(h * RET_DK, (h + 1) * RET_DK)
            v_cols = slice(h * RET_DV, (h + 1) * RET_DV)
            q = rotate(q_ref[rows, qk_cols].astype(F32), cos, sin)
            k = rotate(k_ref[rows, qk_cols].astype(F32), cos, sin) * (RET_DK ** -0.5)
            v = v_ref[rows, v_cols]
            qb = q.astype(BF16)
            s = _dot_nt(qb, k.astype(BF16)) * decay_ref[h]
            state = state_ref[h]
            o = _dot(s.astype(BF16), v) + _dot(qb, state.astype(BF16)) * xi_ref[h]
            kz_t = (k * zeta_ref[h]).T.astype(BF16)
            state_ref[h] = g_ref[h] * state + _dot(kz_t, v)
            o = o * lax.rsqrt(jnp.mean(o * o, axis=-1, keepdims=True) + EPS)
            o_ref[rows, v_cols] = (o * _silu(gate_ref[rows, v_cols].astype(F32))).astype(o_ref.dtype)
        return carry

    lax.fori_loop(0, q_ref.shape[0] // RET_CHUNK, chunk_body, 0)


def _rotary_tables(S):
    half = RET_DK // 2
    inv = 1.0 / (RET_THETA_BASE ** jnp.linspace(0.0, 1.0, half, dtype=F32))
    ang = jnp.arange(S, dtype=F32)[:, None] * inv[None, :]
    cos, sin = jnp.cos(ang), jnp.sin(ang)
    return jnp.concatenate([cos, cos], axis=-1), jnp.concatenate([-sin, sin], axis=-1)


def _decay_tables():
    H, C = RET_HEADS, RET_CHUNK
    log_g = jnp.log1p(-(2.0 ** (-5.0 - jnp.arange(H, dtype=F32))))
    pos = jnp.arange(C, dtype=F32)
    rel = pos[:, None] - pos[None, :]
    decay_in = jnp.where(rel >= 0, jnp.exp(jnp.maximum(rel, 0.0)[None] * log_g[:, None, None]), 0.0)
    xi = jnp.exp((pos[None, :] + 1.0) * log_g[:, None])
    zeta = jnp.exp((C - 1.0 - pos[None, :]) * log_g[:, None])
    g_chunk = jnp.exp(C * log_g)
    return g_chunk, decay_in, xi[:, :, None], zeta[:, :, None]


def retention_gated(proj, B, S):
    M = B * S
    tq = RET_TQ
    spb = S // tq
    g_chunk, decay_in, xi, zeta = _decay_tables()
    cos, sin = _rotary_tables(S)
    k_blk = 1
    v_blk = 2 * RET_QK_W // RET_V_W
    gate_blk = v_blk + 1
    row = lambda b, i: b * spb + i
    const3 = lambda b, i: (0, 0, 0)
    return pl.pallas_call(
        _retention_kernel,
        grid=(B, spb),
        in_specs=[
            pl.BlockSpec(memory_space=pltpu.SMEM),
            pl.BlockSpec((tq, RET_QK_W), lambda b, i: (row(b, i), 0)),
            pl.BlockSpec((tq, RET_QK_W), lambda b, i: (row(b, i), k_blk)),
            pl.BlockSpec((tq, RET_V_W), lambda b, i: (row(b, i), v_blk)),
            pl.BlockSpec((tq, RET_V_W), lambda b, i: (row(b, i), gate_blk)),
            pl.BlockSpec((tq, RET_DK), lambda b, i: (i, 0)),
            pl.BlockSpec((tq, RET_DK), lambda b, i: (i, 0)),
            pl.BlockSpec((RET_HEADS, RET_CHUNK, RET_CHUNK), const3),
            pl.BlockSpec((RET_HEADS, RET_CHUNK, 1), const3),
            pl.BlockSpec((RET_HEADS, RET_CHUNK, 1), const3),
        ],
        out_specs=pl.BlockSpec((tq, RET_V_W), lambda b, i: (row(b, i), 0)),
        out_shape=jax.ShapeDtypeStruct((M, RET_V_W), BF16),
        scratch_shapes=[pltpu.VMEM((RET_HEADS, RET_DK, RET_DV), F32)],
        compiler_params=_params(("parallel", "arbitrary")),
        name="retention",
    )(g_chunk, proj, proj, proj, proj, cos, sin, decay_in, xi, zeta)


def _mem_attn_kernel(q_ref, gate_ref, k_ref, v_ref, o_ref):
    for h in range(MEM_HEADS):
        cols = slice(h * MEM_DH, (h + 1) * MEM_DH)
        s = _dot_nt(q_ref[:, cols], k_ref[:, cols]) * (MEM_DH ** -0.5)
        p = jnp.exp(s - jnp.max(s, axis=-1, keepdims=True))
        l = jnp.sum(p, axis=-1, keepdims=True)
        o = _dot(p.astype(BF16), v_ref[:, cols]) * (1.0 / l)
        o_ref[:, cols] = (o * _silu(gate_ref[:, cols].astype(F32))).astype(o_ref.dtype)


def mem_attention_gated(proj, mem_kv, B, S, q_blk):
    M = B * S
    tq = MEM_TQ
    spb = S // tq
    row = lambda b, i: b * spb + i
    return pl.pallas_call(
        _mem_attn_kernel,
        grid=(B, spb),
        in_specs=[
            pl.BlockSpec((tq, MEM_W), lambda b, i: (row(b, i), q_blk)),
            pl.BlockSpec((tq, MEM_W), lambda b, i: (row(b, i), q_blk + 1)),
            pl.BlockSpec((MEM_LEN, MEM_W), lambda b, i: (b, 0)),
            pl.BlockSpec((MEM_LEN, MEM_W), lambda b, i: (b, 1)),
        ],
        out_specs=pl.BlockSpec((tq, MEM_W), lambda b, i: (row(b, i), 0)),
        out_shape=jax.ShapeDtypeStruct((M, MEM_W), BF16),
        compiler_params=_params(("parallel", "parallel")),
        name="mem_attention",
    )(proj, proj, mem_kv, mem_kv)


DIL_TOK = max(d for _, d in DIL_CONFIG) * DIL_BLOCK
DIL_KEYS = 2 * DIL_BLOCK


def _residue_rows(dil):
    per_class = DIL_TOK // dil
    return per_class, DIL_BLOCK + per_class


def _class_pieces(dil, r, pos0, count):
    per_perm = PERM_TOK // dil
    pieces = []
    while count:
        blk, off = divmod(pos0, per_perm)
        size = min(count, per_perm - off)
        pieces.append((blk * PERM_TOK + r * per_perm + off, size))
        pos0, count = pos0 + size, count - size
    return pieces


def _dilated_kernel(q0_ref, q1_ref, q2_ref, gate_ref, k0_ref, v0_ref, k1_ref, v1_ref, k2_ref, v2_ref, o_ref,
                    ke0_ref, ve0_ref, ke1_ref, ve1_ref, ke2_ref, ve2_ref, acc_ref, m_ref, l_ref):
    j = pl.program_id(2)
    q_refs = (q0_ref, q1_ref, q2_ref)
    kv_refs = ((k0_ref, v0_ref, ke0_ref, ve0_ref), (k1_ref, v1_ref, ke1_ref, ve1_ref), (k2_ref, v2_ref, ke2_ref, ve2_ref))

    for (k_ref, v_ref, ke_ref, ve_ref), (_, dil) in zip(kv_refs, DIL_CONFIG):
        per_class, seg = _residue_rows(dil)

        @pl.when(j == 0)
        def _zero_halo(ke_ref=ke_ref, ve_ref=ve_ref, dil=dil, seg=seg):
            for r in range(dil):
                ke_ref[r * seg:r * seg + DIL_BLOCK, :] = jnp.zeros((DIL_BLOCK, DIL_DH), BF16)
                ve_ref[r * seg:r * seg + DIL_BLOCK, :] = jnp.zeros((DIL_BLOCK, DIL_DH), BF16)

        @pl.when(j > 0)
        def _carry_halo(ke_ref=ke_ref, ve_ref=ve_ref, dil=dil, seg=seg, per_class=per_class):
            for r in range(dil):
                ke_ref[r * seg:r * seg + DIL_BLOCK, :] = ke_ref[r * seg + per_class:(r + 1) * seg, :]
                ve_ref[r * seg:r * seg + DIL_BLOCK, :] = ve_ref[r * seg + per_class:(r + 1) * seg, :]

        for r in range(dil):
            dst = r * seg + DIL_BLOCK
            for start, size in _class_pieces(dil, r, 0, per_class):
                ke_ref[dst:dst + size, :] = k_ref[start:start + size, :]
                ve_ref[dst:dst + size, :] = v_ref[start:start + size, :]
                dst += size

    qi = lax.broadcasted_iota(jnp.int32, (DIL_BLOCK, DIL_KEYS), 0)
    kk = lax.broadcasted_iota(jnp.int32, (DIL_BLOCK, DIL_KEYS), 1)
    band = (kk >= qi) & (kk <= qi + DIL_BLOCK)
    band_first = band & (kk >= jnp.where(j == 0, DIL_BLOCK, 0))
    scale = DIL_DH ** -0.5

    def attend(q, ke_ref, ve_ref, key0, mask, tok, first_group):
        kc = ke_ref[pl.ds(key0, DIL_KEYS), :]
        vc = ve_ref[pl.ds(key0, DIL_KEYS), :]
        s = jnp.where(mask, _dot_nt(q, kc) * scale, -1e30)
        m_blk = jnp.max(s, axis=-1, keepdims=True)
        if first_group:
            m_new = jnp.broadcast_to(m_blk, (DIL_BLOCK, DIL_DH))
        else:
            m_old = m_ref[tok, :]
            m_new = jnp.maximum(m_old, m_blk)
            alpha = jnp.exp(m_old - m_new)
        p = jnp.exp(s - jnp.concatenate([m_new, m_new], axis=1))
        l_blk = jnp.sum(p, axis=-1, keepdims=True)
        pv = _dot(p.astype(BF16), vc)
        if first_group:
            l_ref[tok, :] = jnp.broadcast_to(l_blk, (DIL_BLOCK, DIL_DH))
            acc_ref[tok, :] = pv
        else:
            l_ref[tok, :] = alpha * l_ref[tok, :] + l_blk
            acc_ref[tok, :] = alpha * acc_ref[tok, :] + pv
        m_ref[tok, :] = m_new

    for blk in range(DIL_TOK // DIL_BLOCK):
        rows = pl.ds(blk * DIL_BLOCK, DIL_BLOCK)
        attend(q0_ref[rows, :], ke0_ref, ve0_ref, blk * DIL_BLOCK, band_first if blk == 0 else band, rows, True)

    for group in range(1, len(DIL_CONFIG)):
        dil = DIL_CONFIG[group][1]
        per_class, seg = _residue_rows(dil)
        _, _, ke_ref, ve_ref = kv_refs[group]
        for r in range(dil):
            for blk in range(per_class // DIL_BLOCK):
                pieces = _class_pieces(dil, r, blk * DIL_BLOCK, DIL_BLOCK)
                q = jnp.concatenate([q_refs[group][start:start + size, :] for start, size in pieces], axis=0)
                tok = pl.ds(blk * DIL_BLOCK * dil + r, DIL_BLOCK, stride=dil)
                attend(q, ke_ref, ve_ref, r * seg + blk * DIL_BLOCK, band_first if blk == 0 else band, tok, False)

    def finish(c, carry):
        rows = pl.ds(pl.multiple_of(c * DIL_BLOCK, DIL_BLOCK), DIL_BLOCK)
        o = acc_ref[rows, :] * (1.0 / l_ref[rows, :]) * _silu(gate_ref[rows, :].astype(F32))
        o_ref[rows, :] = o.astype(o_ref.dtype)
        return carry

    lax.fori_loop(0, DIL_TOK // DIL_BLOCK, finish, 0)


def dilated_gated(proj, kv, B, S):
    assert S % DIL_TOK == 0
    nblk = S // DIL_TOK
    heads_per_group = DIL_W // DIL_DH

    def head_cols(group_col):
        return pl.BlockSpec((DIL_TOK, DIL_DH), lambda b, h, j: (b * nblk + j, group_col * heads_per_group + h))

    ext = lambda dil: pltpu.VMEM((dil * _residue_rows(dil)[1], DIL_DH), BF16)
    tok_f32 = pltpu.VMEM((DIL_TOK, DIL_DH), F32)
    return pl.pallas_call(
        _dilated_kernel,
        grid=(B, DIL_HEADS, nblk),
        in_specs=[head_cols(c) for c in range(4)] + [head_cols(c) for c in range(6)],
        out_specs=pl.BlockSpec((DIL_TOK, DIL_DH), lambda b, h, j: (b * nblk + j, h)),
        out_shape=jax.ShapeDtypeStruct((B * S, DIL_W), BF16),
        scratch_shapes=[ext(d) for _, d in DIL_CONFIG for _ in range(2)] + [tok_f32, tok_f32, tok_f32],
        compiler_params=_params(("arbitrary", "arbitrary", "arbitrary")),
        name="dilated_attention",
    )(proj, proj, proj, proj, kv, kv, kv, kv, kv, kv)


def kernel(x, mem, norm_a, w_in_a, w_out_a, norm_b, w_in_b, w_out_b, w_mem_kv, mem_norm_g, kv_norm_g, w_kv, final_norm_g):
    B, S, D = x.shape
    xr = x.reshape(B * S, D)
    mem_r = mem.reshape(B * mem.shape[1], D)

    w_in_a, w_out_a, w_in_b, w_out_b, w_mem_kv = (t.astype(BF16) for t in (w_in_a, w_out_a, w_in_b, w_out_b, w_mem_kv))
    w_kv = w_kv.astype(BF16)[None]
    norm_a = norm_a[:, None, :]
    norm_b = norm_b[:, None, :]
    mem_g = mem_norm_g[None, None, :]
    kv_g = kv_norm_g[None, None, :]

    dils = tuple(d for _, d in DIL_CONFIG)
    tiles_per_group = DIL_W // PROJ_TN
    group_copies = tuple(range(len(DIL_CONFIG)))
    in_b_copies = tuple(c for c in group_copies + (0,) * ((IN_W_B - len(DIL_CONFIG) * DIL_W) // DIL_W)
                        for _ in range(tiles_per_group))
    kv_copies = tuple(c for c in group_copies for _ in range(2 * tiles_per_group))

    kv = None
    for layer in range(DEPTH):
        mem_kv = norm_matmul(mem_r, mem_g, 0, w_mem_kv, layer, name=f"mem_kv_proj{layer}")
        if layer < N_A:
            proj = norm_matmul(xr, norm_a, layer, w_in_a, layer, name=f"in_proj_a{layer}")
            o_main = retention_gated(proj, B, S)
            o_mem = mem_attention_gated(proj, mem_kv, B, S, (IN_W_A - 2 * MEM_W) // MEM_W)
            xr = out_proj(o_main, o_mem, w_out_a, layer, xr, name=f"out_proj_a{layer}")
            if layer == N_A - 1:
                kv = norm_matmul(xr, kv_g, 0, w_kv, 0, name="shared_kv_proj", dils=dils, tile_copy=kv_copies)
        else:
            jb = layer - N_A
            proj = norm_matmul(xr, norm_b, jb, w_in_b, jb, name=f"in_proj_b{jb}", dils=dils, tile_copy=in_b_copies)
            o_main = dilated_gated(proj, kv, B, S)
            o_mem = mem_attention_gated(proj, mem_kv, B, S, (IN_W_B - 2 * MEM_W) // MEM_W)
            xr = out_proj(o_main, o_mem, w_out_b, jb, xr, name=f"out_proj_b{jb}")
    return rmsnorm_rows(xr, final_norm_g[None, :]).reshape(B, S, D)
```

```python
import functools
import math

import jax
import jax.numpy as jnp
from jax import lax
from jax.experimental import pallas as pl
from jax.experimental.pallas import tpu as pltpu

F32 = jnp.float32
BF16 = jnp.bfloat16

D_MODEL = 2048
DEPTH = 4
N_A = DEPTH // 2
MEM_LEN = 256
RET_HEADS = 8
RET_DK = D_MODEL // 16
RET_DV = 2 * RET_DK
RET_CHUNK = 128
RET_THETA_BASE = 10000.0
DIL_CONFIG = ((128, 1), (512, 4), (2048, 16))
DIL_HEADS = 16
DIL_DH = D_MODEL // 16
DIL_BLOCK = 128
MEM_HEADS = 4
MEM_DH = D_MODEL // 8
EPS = 1e-6

RET_QK_W = RET_HEADS * RET_DK
RET_V_W = RET_HEADS * RET_DV
DIL_W = DIL_HEADS * DIL_DH
MEM_W = MEM_HEADS * MEM_DH
IN_W_A = 2 * RET_QK_W + 2 * RET_V_W + 2 * MEM_W
IN_W_B = len(DIL_CONFIG) * DIL_W + DIL_W + 2 * MEM_W
KV_W = 2 * len(DIL_CONFIG) * DIL_W

LANES = 128
VMEM_LIMIT_BYTES = 56 * 2**20
NORM_ROW_CHUNK = 64
PROJ_TM, PROJ_TN = 1024, 1024
PERM_TOK = 256
RET_TQ = 512
MEM_TQ = 512
FINAL_TM = 512


def _params(semantics):
    return pltpu.CompilerParams(dimension_semantics=semantics, vmem_limit_bytes=VMEM_LIMIT_BYTES)


def _dot(a, b):
    return jnp.dot(a, b, preferred_element_type=F32)


def _dot_nt(a, b):
    return lax.dot_general(a, b, (((1,), (1,)), ((), ())), preferred_element_type=F32)


def _silu(x):
    return x * (1.0 / (1.0 + jnp.exp(-x)))


def _residue_order_matrix(dil):
    per_class = PERM_TOK // dil
    dst = lax.broadcasted_iota(jnp.int32, (PERM_TOK, PERM_TOK), 0)
    src = lax.broadcasted_iota(jnp.int32, (PERM_TOK, PERM_TOK), 1)
    wanted = (dst & (per_class - 1)) * dil + (dst >> int(math.log2(per_class)))
    return jnp.where(src == wanted, 1.0, 0.0).astype(BF16)


def _norm_matmul_kernel(x_ref, g_ref, w_ref, o_ref, xn_ref, *, dils, tile_copy):
    j = pl.program_id(1)

    @pl.when(j == 0)
    def _normalize():
        def body(c, carry):
            r = pl.multiple_of(c * NORM_ROW_CHUNK, NORM_ROW_CHUNK)
            xc = x_ref[pl.ds(r, NORM_ROW_CHUNK), :]
            ms = jnp.mean(xc * xc, axis=-1, keepdims=True)
            xn_ref[0, pl.ds(r, NORM_ROW_CHUNK), :] = (xc * lax.rsqrt(ms + EPS) * g_ref[...]).astype(BF16)
            return carry

        lax.fori_loop(0, x_ref.shape[0] // NORM_ROW_CHUNK, body, 0, unroll=2)
        for v in range(1, len(dils)):
            order = _residue_order_matrix(dils[v])
            for b in range(x_ref.shape[0] // PERM_TOK):
                rows = slice(b * PERM_TOK, (b + 1) * PERM_TOK)
                xn_ref[v, rows, :] = _dot(order, xn_ref[0, rows, :]).astype(BF16)

    copy = 0
    for t, v in enumerate(tile_copy):
        if v:
            copy = copy + v * (j == t).astype(jnp.int32)
    o_ref[...] = _dot(xn_ref[copy], w_ref[...]).astype(o_ref.dtype)


def norm_matmul(x, g, g_idx, w, w_idx, *, name, dils=(1,), tile_copy=None):
    M, K = x.shape
    N = w.shape[-1]
    tm, tn = min(PROJ_TM, M), min(PROJ_TN, N)
    tile_copy = tile_copy or (0,) * (N // tn)
    assert dils[0] == 1 and len(tile_copy) == N // tn and tm % PERM_TOK == 0
    return pl.pallas_call(
        functools.partial(_norm_matmul_kernel, dils=dils, tile_copy=tile_copy),
        grid=(M // tm, N // tn),
        in_specs=[
            pl.BlockSpec((tm, K), lambda i, j: (i, 0)),
            pl.BlockSpec((None, 1, K), lambda i, j: (g_idx, 0, 0)),
            pl.BlockSpec((None, K, tn), lambda i, j: (w_idx, 0, j)),
        ],
        out_specs=pl.BlockSpec((tm, tn), lambda i, j: (i, j)),
        out_shape=jax.ShapeDtypeStruct((M, N), BF16),
        scratch_shapes=[pltpu.VMEM((len(dils), tm, K), BF16)],
        compiler_params=_params(("parallel", "arbitrary")),
        name=name,
    )(x, g, w)


def _out_proj_kernel(om_ref, omem_ref, w_ref, x_ref, o_ref):
    k_main = om_ref.shape[1]
    acc = _dot(om_ref[...], w_ref[:k_main, :]) + _dot(omem_ref[...], w_ref[k_main:, :])
    o_ref[...] = x_ref[...] + acc


def out_proj(o_main, o_mem, w, layer, x, *, name):
    M, k_main = o_main.shape
    k_mem = o_mem.shape[1]
    N = w.shape[-1]
    tm, tn = PROJ_TM, PROJ_TN
    return pl.pallas_call(
        _out_proj_kernel,
        grid=(M // tm, N // tn),
        in_specs=[
            pl.BlockSpec((tm, k_main), lambda i, j: (i, 0)),
            pl.BlockSpec((tm, k_mem), lambda i, j: (i, 0)),
            pl.BlockSpec((None, k_main + k_mem, tn), lambda i, j: (layer, 0, j)),
            pl.BlockSpec((tm, tn), lambda i, j: (i, j)),
        ],
        out_specs=pl.BlockSpec((tm, tn), lambda i, j: (i, j)),
        out_shape=jax.ShapeDtypeStruct((M, N), F32),
        compiler_params=_params(("parallel", "parallel")),
        name=name,
    )(o_main, o_mem, w, x)


def _rmsnorm_kernel(x_ref, g_ref, o_ref):
    def body(c, carry):
        r = pl.multiple_of(c * NORM_ROW_CHUNK, NORM_ROW_CHUNK)
        xc = x_ref[pl.ds(r, NORM_ROW_CHUNK), :]
        ms = jnp.mean(xc * xc, axis=-1, keepdims=True)
        o_ref[pl.ds(r, NORM_ROW_CHUNK), :] = xc * lax.rsqrt(ms + EPS) * g_ref[...]
        return carry

    lax.fori_loop(0, x_ref.shape[0] // NORM_ROW_CHUNK, body, 0)


def rmsnorm_rows(x, g):
    M, K = x.shape
    tm = FINAL_TM
    return pl.pallas_call(
        _rmsnorm_kernel,
        grid=(M // tm,),
        in_specs=[pl.BlockSpec((tm, K), lambda i: (i, 0)), pl.BlockSpec((1, K), lambda i: (0, 0))],
        out_specs=pl.BlockSpec((tm, K), lambda i: (i, 0)),
        out_shape=jax.ShapeDtypeStruct((M, K), F32),
        compiler_params=_params(("parallel",)),
        name="final_rmsnorm",
    )(x, g)


def _retention_kernel(g_ref, q_ref, k_ref, v_ref, gate_ref, cos_ref, sin_ref, decay_ref, xi_ref, zeta_ref,
                      o_ref, state_ref):
    @pl.when(pl.program_id(1) == 0)
    def _reset():
        state_ref[...] = jnp.zeros_like(state_ref)

    def rotate(t, cos, sin):
        return t * cos + pltpu.roll(t, RET_DK // 2, axis=1) * sin

    for c in range(q_ref.shape[0] // RET_CHUNK):
        rows = pl.ds(c * RET_CHUNK, RET_CHUNK)
        cos = cos_ref[rows, :]
        sin = sin_ref[rows, :]
        for h in range(RET_HEADS):
            qk_cols = slice(h * RET_DK, (h + 1) * RET_DK)
            v_cols = slice(h * RET_DV, (h + 1) * RET_DV)
            q = rotate(q_ref[rows, qk_cols].astype(F32), cos, sin)
            k = rotate(k_ref[rows, qk_cols].astype(F32), cos, sin) * (RET_DK ** -0.5)
            v = v_ref[rows, v_cols]
            qb = q.astype(BF16)
            s = _dot_nt(qb, k.astype(BF16)) * decay_ref[h]
            state = state_ref[h]
            o = _dot(s.astype(BF16), v) + _dot(qb, state.astype(BF16)) * xi_ref[h]
            kz_t = (k * zeta_ref[h]).T.astype(BF16)
            state_ref[h] = g_ref[h] * state + _dot(kz_t, v)
            o = o * lax.rsqrt(jnp.mean(o * o, axis=-1, keepdims=True) + EPS)
            o_ref[rows, v_cols] = (o * _silu(gate_ref[rows, v_cols].astype(F32))).astype(o_ref.dtype)


def _rotary_tables(S):
    half = RET_DK // 2
    inv = 1.0 / (RET_THETA_BASE ** jnp.linspace(0.0, 1.0, half, dtype=F32))
    ang = jnp.arange(S, dtype=F32)[:, None] * inv[None, :]
    cos, sin = jnp.cos(ang), jnp.sin(ang)
    return jnp.concatenate([cos, cos], axis=-1), jnp.concatenate([-sin, sin], axis=-1)


def _decay_tables():
    H, C = RET_HEADS, RET_CHUNK
    log_g = jnp.log1p(-(2.0 ** (-5.0 - jnp.arange(H, dtype=F32))))
    pos = jnp.arange(C, dtype=F32)
    rel = pos[:, None] - pos[None, :]
    decay_in = jnp.where(rel >= 0, jnp.exp(jnp.maximum(rel, 0.0)[None] * log_g[:, None, None]), 0.0)
    xi = jnp.exp((pos[None, :] + 1.0) * log_g[:, None])
    zeta = jnp.exp((C - 1.0 - pos[None, :]) * log_g[:, None])
    g_chunk = jnp.exp(C * log_g)
    return g_chunk, decay_in, xi[:, :, None], zeta[:, :, None]


def retention_gated(proj, B, S):
    M = B * S
    tq = RET_TQ
    spb = S // tq
    g_chunk, decay_in, xi, zeta = _decay_tables()
    cos, sin = _rotary_tables(S)
    k_blk = 1
    v_blk = 2 * RET_QK_W // RET_V_W
    gate_blk = v_blk + 1
    row = lambda b, i: b * spb + i
    const3 = lambda b, i: (0, 0, 0)
    return pl.pallas_call(
        _retention_kernel,
        grid=(B, spb),
        in_specs=[
            pl.BlockSpec(memory_space=pltpu.SMEM),
            pl.BlockSpec((tq, RET_QK_W), lambda b, i: (row(b, i), 0)),
            pl.BlockSpec((tq, RET_QK_W), lambda b, i: (row(b, i), k_blk)),
            pl.BlockSpec((tq, RET_V_W), lambda b, i: (row(b, i), v_blk)),
            pl.BlockSpec((tq, RET_V_W), lambda b, i: (row(b, i), gate_blk)),
            pl.BlockSpec((tq, RET_DK), lambda b, i: (i, 0)),
            pl.BlockSpec((tq, RET_DK), lambda b, i: (i, 0)),
            pl.BlockSpec((RET_HEADS, RET_CHUNK, RET_CHUNK), const3),
            pl.BlockSpec((RET_HEADS, RET_CHUNK, 1), const3),
            pl.BlockSpec((RET_HEADS, RET_CHUNK, 1), const3),
        ],
        out_specs=pl.BlockSpec((tq, RET_V_W), lambda b, i: (row(b, i), 0)),
        out_shape=jax.ShapeDtypeStruct((M, RET_V_W), BF16),
        scratch_shapes=[pltpu.VMEM((RET_HEADS, RET_DK, RET_DV), F32)],
        compiler_params=_params(("parallel", "arbitrary")),
        name="retention",
    )(g_chunk, proj, proj, proj, proj, cos, sin, decay_in, xi, zeta)


def _mem_attn_kernel(q_ref, gate_ref, k_ref, v_ref, o_ref):
    for h in range(MEM_HEADS):
        cols = slice(h * MEM_DH, (h + 1) * MEM_DH)
        s = _dot_nt(q_ref[:, cols], k_ref[:, cols]) * (MEM_DH ** -0.5)
        p = jnp.exp(s - jnp.max(s, axis=-1, keepdims=True))
        l = jnp.sum(p, axis=-1, keepdims=True)
        o = _dot(p.astype(BF16), v_ref[:, cols]) * (1.0 / l)
        o_ref[:, cols] = (o * _silu(gate_ref[:, cols].astype(F32))).astype(o_ref.dtype)


def mem_attention_gated(proj, mem_kv, B, S, q_blk):
    M = B * S
    tq = MEM_TQ
    spb = S // tq
    row = lambda b, i: b * spb + i
    return pl.pallas_call(
        _mem_attn_kernel,
        grid=(B, spb),
        in_specs=[
            pl.BlockSpec((tq, MEM_W), lambda b, i: (row(b, i), q_blk)),
            pl.BlockSpec((tq, MEM_W), lambda b, i: (row(b, i), q_blk + 1)),
            pl.BlockSpec((MEM_LEN, MEM_W), lambda b, i: (b, 0)),
            pl.BlockSpec((MEM_LEN, MEM_W), lambda b, i: (b, 1)),
        ],
        out_specs=pl.BlockSpec((tq, MEM_W), lambda b, i: (row(b, i), 0)),
        out_shape=jax.ShapeDtypeStruct((M, MEM_W), BF16),
        compiler_params=_params(("parallel", "parallel")),
        name="mem_attention",
    )(proj, proj, mem_kv, mem_kv)


DIL_TOK = max(d for _, d in DIL_CONFIG) * DIL_BLOCK
DIL_KEYS = 2 * DIL_BLOCK


def _residue_rows(dil):
    per_class = DIL_TOK // dil
    return per_class, DIL_BLOCK + per_class


def _class_pieces(dil, r, pos0, count):
    per_perm = PERM_TOK // dil
    pieces = []
    while count:
        blk, off = divmod(pos0, per_perm)
        size = min(count, per_perm - off)
        pieces.append((blk * PERM_TOK + r * per_perm + off, size))
        pos0, count = pos0 + size, count - size
    return pieces


def _dilated_kernel(q0_ref, q1_ref, q2_ref, gate_ref, k0_ref, v0_ref, k1_ref, v1_ref, k2_ref, v2_ref, o_ref,
                    ke0_ref, ve0_ref, ke1_ref, ve1_ref, ke2_ref, ve2_ref, acc_ref, m_ref, l_ref):
    j = pl.program_id(2)
    q_refs = (q0_ref, q1_ref, q2_ref)
    kv_refs = ((k0_ref, v0_ref, ke0_ref, ve0_ref), (k1_ref, v1_ref, ke1_ref, ve1_ref), (k2_ref, v2_ref, ke2_ref, ve2_ref))

    for (k_ref, v_ref, ke_ref, ve_ref), (_, dil) in zip(kv_refs, DIL_CONFIG):
        per_class, seg = _residue_rows(dil)

        @pl.when(j == 0)
        def _zero_halo(ke_ref=ke_ref, ve_ref=ve_ref, dil=dil, seg=seg):
            for r in range(dil):
                ke_ref[r * seg:r * seg + DIL_BLOCK, :] = jnp.zeros((DIL_BLOCK, DIL_DH), BF16)
                ve_ref[r * seg:r * seg + DIL_BLOCK, :] = jnp.zeros((DIL_BLOCK, DIL_DH), BF16)

        @pl.when(j > 0)
        def _carry_halo(ke_ref=ke_ref, ve_ref=ve_ref, dil=dil, seg=seg, per_class=per_class):
            for r in range(dil):
                ke_ref[r * seg:r * seg + DIL_BLOCK, :] = ke_ref[r * seg + per_class:(r + 1) * seg, :]
                ve_ref[r * seg:r * seg + DIL_BLOCK, :] = ve_ref[r * seg + per_class:(r + 1) * seg, :]

        for r in range(dil):
            dst = r * seg + DIL_BLOCK
            for start, size in _class_pieces(dil, r, 0, per_class):
                ke_ref[dst:dst + size, :] = k_ref[start:start + size, :]
                ve_ref[dst:dst + size, :] = v_ref[start:start + size, :]
                dst += size

    qi = lax.broadcasted_iota(jnp.int32, (DIL_BLOCK, DIL_KEYS), 0)
    kk = lax.broadcasted_iota(jnp.int32, (DIL_BLOCK, DIL_KEYS), 1)
    band = (kk >= qi) & (kk <= qi + DIL_BLOCK)
    band_first = band & (kk >= jnp.where(j == 0, DIL_BLOCK, 0))
    exp2_scale = DIL_DH ** -0.5 * math.log2(math.e)

    def attend(q, ke_ref, ve_ref, key0, mask, tok, first_group):
        kc = ke_ref[pl.ds(key0, DIL_KEYS), :]
        vc = ve_ref[pl.ds(key0, DIL_KEYS), :]
        s = jnp.where(mask, _dot_nt(q, kc), -1e30)
        m_blk = jnp.max(s, axis=-1, keepdims=True)
        if first_group:
            m_new = jnp.broadcast_to(m_blk, (DIL_BLOCK, DIL_DH))
        else:
            m_old = m_ref[tok, :]
            m_new = jnp.maximum(m_old, m_blk)
            alpha = jnp.exp2((m_old - m_new) * exp2_scale)
        p = jnp.exp2((s - jnp.concatenate([m_new, m_new], axis=1)) * exp2_scale)
        l_blk = jnp.sum(p, axis=-1, keepdims=True)
        pv = _dot(p.astype(BF16), vc)
        if first_group:
            l_ref[tok, :] = jnp.broadcast_to(l_blk, (DIL_BLOCK, DIL_DH))
            acc_ref[tok, :] = pv
        else:
            l_ref[tok, :] = alpha * l_ref[tok, :] + l_blk
            acc_ref[tok, :] = alpha * acc_ref[tok, :] + pv
        m_ref[tok, :] = m_new

    for blk in range(DIL_TOK // DIL_BLOCK):
        rows = pl.ds(blk * DIL_BLOCK, DIL_BLOCK)
        attend(q0_ref[rows, :], ke0_ref, ve0_ref, blk * DIL_BLOCK, band_first if blk == 0 else band, rows, True)

    for group in range(1, len(DIL_CONFIG)):
        dil = DIL_CONFIG[group][1]
        per_class, seg = _residue_rows(dil)
        _, _, ke_ref, ve_ref = kv_refs[group]
        for r in range(dil):
            for blk in range(per_class // DIL_BLOCK):
                pieces = _class_pieces(dil, r, blk * DIL_BLOCK, DIL_BLOCK)
                q = jnp.concatenate([q_refs[group][start:start + size, :] for start, size in pieces], axis=0)
                tok = pl.ds(blk * DIL_BLOCK * dil + r, DIL_BLOCK, stride=dil)
                attend(q, ke_ref, ve_ref, r * seg + blk * DIL_BLOCK, band_first if blk == 0 else band, tok, False)

    def finish(c, carry):
        rows = pl.ds(pl.multiple_of(c * DIL_BLOCK, DIL_BLOCK), DIL_BLOCK)
        o = acc_ref[rows, :] * (1.0 / l_ref[rows, :]) * _silu(gate_ref[rows, :].astype(F32))
        o_ref[rows, :] = o.astype(o_ref.dtype)
        return carry

    lax.fori_loop(0, DIL_TOK // DIL_BLOCK, finish, 0)


def dilated_gated(proj, kv, B, S):
    assert S % DIL_TOK == 0
    nblk = S // DIL_TOK
    heads_per_group = DIL_W // DIL_DH

    def head_cols(group_col):
        return pl.BlockSpec((DIL_TOK, DIL_DH), lambda b, h, j: (b * nblk + j, group_col * heads_per_group + h))

    ext = lambda dil: pltpu.VMEM((dil * _residue_rows(dil)[1], DIL_DH), BF16)
    tok_f32 = pltpu.VMEM((DIL_TOK, DIL_DH), F32)
    return pl.pallas_call(
        _dilated_kernel,
        grid=(B, DIL_HEADS, nblk),
        in_specs=[head_cols(c) for c in range(4)] + [head_cols(c) for c in range(6)],
        out_specs=pl.BlockSpec((DIL_TOK, DIL_DH), lambda b, h, j: (b * nblk + j, h)),
        out_shape=jax.ShapeDtypeStruct((B * S, DIL_W), BF16),
        scratch_shapes=[ext(d) for _, d in DIL_CONFIG for _ in range(2)] + [tok_f32, tok_f32, tok_f32],
        compiler_params=_params(("arbitrary", "arbitrary", "arbitrary")),
        name="dilated_attention",
    )(proj, proj, proj, proj, kv, kv, kv, kv, kv, kv)


def kernel(x, mem, norm_a, w_in_a, w_out_a, norm_b, w_in_b, w_out_b, w_mem_kv, mem_norm_g, kv_norm_g, w_kv, final_norm_g):
    B, S, D = x.shape
    xr = x.reshape(B * S, D)
    mem_r = mem.reshape(B * mem.shape[1], D)

    w_in_a, w_out_a, w_in_b, w_out_b, w_mem_kv = (t.astype(BF16) for t in (w_in_a, w_out_a, w_in_b, w_out_b, w_mem_kv))
    w_kv = w_kv.astype(BF16)[None]
    norm_a = norm_a[:, None, :]
    norm_b = norm_b[:, None, :]
    mem_g = mem_norm_g[None, None, :]
    kv_g = kv_norm_g[None, None, :]

    dils = tuple(d for _, d in DIL_CONFIG)
    tiles_per_group = DIL_W // PROJ_TN
    group_copies = tuple(range(len(DIL_CONFIG)))
    in_b_copies = tuple(c for c in group_copies + (0,) * ((IN_W_B - len(DIL_CONFIG) * DIL_W) // DIL_W)
                        for _ in range(tiles_per_group))
    kv_copies = tuple(c for c in group_copies for _ in range(2 * tiles_per_group))

    kv = None
    for layer in range(DEPTH):
        mem_kv = norm_matmul(mem_r, mem_g, 0, w_mem_kv, layer, name=f"mem_kv_proj{layer}")
        if layer < N_A:
            proj = norm_matmul(xr, norm_a, layer, w_in_a, layer, name=f"in_proj_a{layer}")
            o_main = retention_gated(proj, B, S)
            o_mem = mem_attention_gated(proj, mem_kv, B, S, (IN_W_A - 2 * MEM_W) // MEM_W)
            xr = out_proj(o_main, o_mem, w_out_a, layer, xr, name=f"out_proj_a{layer}")
            if layer == N_A - 1:
                kv = norm_matmul(xr, kv_g, 0, w_kv, 0, name="shared_kv_proj", dils=dils, tile_copy=kv_copies)
        else:
            jb = layer - N_A
            proj = norm_matmul(xr, norm_b, jb, w_in_b, jb, name=f"in_proj_b{jb}", dils=dils, tile_copy=in_b_copies)
            o_main = dilated_gated(proj, kv, B, S)
            o_mem = mem_attention_gated(proj, mem_kv, B, S, (IN_W_B - 2 * MEM_W) // MEM_W)
            xr = out_proj(o_main, o_mem, w_out_b, jb, xr, name=f"out_proj_b{jb}")
    return rmsnorm_rows(xr, final_norm_g[None, :]).reshape(B, S, D)
```

```python
import functools
import math

import jax
import jax.numpy as jnp
from jax import lax
from jax.experimental import pallas as pl
from jax.experimental.pallas import tpu as pltpu

F32 = jnp.float32
BF16 = jnp.bfloat16

D_MODEL = 2048
DEPTH = 4
N_A = DEPTH // 2
MEM_LEN = 256
RET_HEADS = 8
RET_DK = D_MODEL // 16
RET_DV = 2 * RET_DK
RET_CHUNK = 128
RET_THETA_BASE = 10000.0
DIL_CONFIG = ((128, 1), (512, 4), (2048, 16))
DIL_HEADS = 16
DIL_DH = D_MODEL // 16
DIL_BLOCK = 128
MEM_HEADS = 4
MEM_DH = D_MODEL // 8
EPS = 1e-6

RET_QK_W = RET_HEADS * RET_DK
RET_V_W = RET_HEADS * RET_DV
DIL_W = DIL_HEADS * DIL_DH
MEM_W = MEM_HEADS * MEM_DH
IN_W_A = 2 * RET_QK_W + 2 * RET_V_W + 2 * MEM_W
IN_W_B = len(DIL_CONFIG) * DIL_W + DIL_W + 2 * MEM_W
KV_W = 2 * len(DIL_CONFIG) * DIL_W

LANES = 128
VMEM_LIMIT_BYTES = 56 * 2**20
NORM_ROW_CHUNK = 64
PROJ_TM, PROJ_TN = 1024, 1024
PERM_TOK = 256
RET_TQ = 512
MEM_TQ = 512
FINAL_TM = 512


def _params(semantics):
    return pltpu.CompilerParams(dimension_semantics=semantics, vmem_limit_bytes=VMEM_LIMIT_BYTES)


def _dot(a, b):
    return jnp.dot(a, b, preferred_element_type=F32)


def _dot_nt(a, b):
    return lax.dot_general(a, b, (((1,), (1,)), ((), ())), preferred_element_type=F32)


def _silu(x):
    return x * (1.0 / (1.0 + jnp.exp(-x)))


def _residue_order_matrix(dil):
    per_class = PERM_TOK // dil
    dst = lax.broadcasted_iota(jnp.int32, (PERM_TOK, PERM_TOK), 0)
    src = lax.broadcasted_iota(jnp.int32, (PERM_TOK, PERM_TOK), 1)
    wanted = (dst & (per_class - 1)) * dil + (dst >> int(math.log2(per_class)))
    return jnp.where(src == wanted, 1.0, 0.0).astype(BF16)


def _norm_matmul_kernel(x_ref, g_ref, w_ref, o_ref, xn_ref, *, dils, tile_copy):
    j = pl.program_id(1)

    @pl.when(j == 0)
    def _normalize():
        def body(c, carry):
            r = pl.multiple_of(c * NORM_ROW_CHUNK, NORM_ROW_CHUNK)
            xc = x_ref[pl.ds(r, NORM_ROW_CHUNK), :]
            ms = jnp.mean(xc * xc, axis=-1, keepdims=True)
            xn_ref[0, pl.ds(r, NORM_ROW_CHUNK), :] = (xc * lax.rsqrt(ms + EPS) * g_ref[...]).astype(BF16)
            return carry

        lax.fori_loop(0, x_ref.shape[0] // NORM_ROW_CHUNK, body, 0, unroll=2)
        for v in range(1, len(dils)):
            order = _residue_order_matrix(dils[v])
            for b in range(x_ref.shape[0] // PERM_TOK):
                rows = slice(b * PERM_TOK, (b + 1) * PERM_TOK)
                xn_ref[v, rows, :] = _dot(order, xn_ref[0, rows, :]).astype(BF16)

    copy = 0
    for t, v in enumerate(tile_copy):
        if v:
            copy = copy + v * (j == t).astype(jnp.int32)
    o_ref[...] = _dot(xn_ref[copy], w_ref[...]).astype(o_ref.dtype)


def norm_matmul(x, g, g_idx, w, w_idx, *, name, dils=(1,), tile_copy=None):
    M, K = x.shape
    N = w.shape[-1]
    tm, tn = min(PROJ_TM, M), min(PROJ_TN, N)
    tile_copy = tile_copy or (0,) * (N // tn)
    assert dils[0] == 1 and len(tile_copy) == N // tn and tm % PERM_TOK == 0
    return pl.pallas_call(
        functools.partial(_norm_matmul_kernel, dils=dils, tile_copy=tile_copy),
        grid=(M // tm, N // tn),
        in_specs=[
            pl.BlockSpec((tm, K), lambda i, j: (i, 0)),
            pl.BlockSpec((None, 1, K), lambda i, j: (g_idx, 0, 0)),
            pl.BlockSpec((None, K, tn), lambda i, j: (w_idx, 0, j)),
        ],
        out_specs=pl.BlockSpec((tm, tn), lambda i, j: (i, j)),
        out_shape=jax.ShapeDtypeStruct((M, N), BF16),
        scratch_shapes=[pltpu.VMEM((len(dils), tm, K), BF16)],
        compiler_params=_params(("parallel", "arbitrary")),
        name=name,
    )(x, g, w)


def _out_proj_kernel(om_ref, omem_ref, w_ref, x_ref, o_ref):
    k_main = om_ref.shape[1]
    acc = _dot(om_ref[...], w_ref[:k_main, :]) + _dot(omem_ref[...], w_ref[k_main:, :])
    o_ref[...] = x_ref[...] + acc


def out_proj(o_main, o_mem, w, layer, x, *, name):
    M, k_main = o_main.shape
    k_mem = o_mem.shape[1]
    N = w.shape[-1]
    tm, tn = PROJ_TM, PROJ_TN
    return pl.pallas_call(
        _out_proj_kernel,
        grid=(M // tm, N // tn),
        in_specs=[
            pl.BlockSpec((tm, k_main), lambda i, j: (i, 0)),
            pl.BlockSpec((tm, k_mem), lambda i, j: (i, 0)),
            pl.BlockSpec((None, k_main + k_mem, tn), lambda i, j: (layer, 0, j)),
            pl.BlockSpec((tm, tn), lambda i, j: (i, j)),
        ],
        out_specs=pl.BlockSpec((tm, tn), lambda i, j: (i, j)),
        out_shape=jax.ShapeDtypeStruct((M, N), F32),
        compiler_params=_params(("parallel", "parallel")),
        name=name,
    )(o_main, o_mem, w, x)


def _rmsnorm_kernel(x_ref, g_ref, o_ref):
    def body(c, carry):
        r = pl.multiple_of(c * NORM_ROW_CHUNK, NORM_ROW_CHUNK)
        xc = x_ref[pl.ds(r, NORM_ROW_CHUNK), :]
        ms = jnp.mean(xc * xc, axis=-1, keepdims=True)
        o_ref[pl.ds(r, NORM_ROW_CHUNK), :] = xc * lax.rsqrt(ms + EPS) * g_ref[...]
        return carry

    lax.fori_loop(0, x_ref.shape[0] // NORM_ROW_CHUNK, body, 0)


def rmsnorm_rows(x, g):
    M, K = x.shape
    tm = FINAL_TM
    return pl.pallas_call(
        _rmsnorm_kernel,
        grid=(M // tm,),
        in_specs=[pl.BlockSpec((tm, K), lambda i: (i, 0)), pl.BlockSpec((1, K), lambda i: (0, 0))],
        out_specs=pl.BlockSpec((tm, K), lambda i: (i, 0)),
        out_shape=jax.ShapeDtypeStruct((M, K), F32),
        compiler_params=_params(("parallel",)),
        name="final_rmsnorm",
    )(x, g)


def _retention_kernel(g_ref, q_ref, k_ref, v_ref, gate_ref, cos_ref, sin_ref, decay_ref, xi_ref, zeta_ref,
                      o_ref, state_ref):
    @pl.when(pl.program_id(1) == 0)
    def _reset():
        state_ref[...] = jnp.zeros_like(state_ref)

    def rotate(t, cos, sin):
        return t * cos + pltpu.roll(t, RET_DK // 2, axis=1) * sin

    for c in range(q_ref.shape[0] // RET_CHUNK):
        rows = pl.ds(c * RET_CHUNK, RET_CHUNK)
        cos = cos_ref[rows, :]
        sin = sin_ref[rows, :]
        for h in range(RET_HEADS):
            qk_cols = slice(h * RET_DK, (h + 1) * RET_DK)
            v_cols = slice(h * RET_DV, (h + 1) * RET_DV)
            q = rotate(q_ref[rows, qk_cols].astype(F32), cos, sin)
            k = rotate(k_ref[rows, qk_cols].astype(F32), cos, sin) * (RET_DK ** -0.5)
            v = v_ref[rows, v_cols]
            qb = q.astype(BF16)
            s = _dot_nt(qb, k.astype(BF16)) * decay_ref[h]
            state = state_ref[h]
            o = _dot(s.astype(BF16), v) + _dot(qb, state.astype(BF16)) * xi_ref[h]
            kz_t = (k * zeta_ref[h]).T.astype(BF16)
            state_ref[h] = g_ref[h] * state + _dot(kz_t, v)
            o = o * lax.rsqrt(jnp.mean(o * o, axis=-1, keepdims=True) + EPS)
            o_ref[rows, v_cols] = (o * _silu(gate_ref[rows, v_cols].astype(F32))).astype(o_ref.dtype)


def _rotary_tables(S):
    half = RET_DK // 2
    inv = 1.0 / (RET_THETA_BASE ** jnp.linspace(0.0, 1.0, half, dtype=F32))
    ang = jnp.arange(S, dtype=F32)[:, None] * inv[None, :]
    cos, sin = jnp.cos(ang), jnp.sin(ang)
    return jnp.concatenate([cos, cos], axis=-1), jnp.concatenate([-sin, sin], axis=-1)


def _decay_tables():
    H, C = RET_HEADS, RET_CHUNK
    log_g = jnp.log1p(-(2.0 ** (-5.0 - jnp.arange(H, dtype=F32))))
    pos = jnp.arange(C, dtype=F32)
    rel = pos[:, None] - pos[None, :]
    decay_in = jnp.where(rel >= 0, jnp.exp(jnp.maximum(rel, 0.0)[None] * log_g[:, None, None]), 0.0)
    xi = jnp.exp((pos[None, :] + 1.0) * log_g[:, None])
    zeta = jnp.exp((C - 1.0 - pos[None, :]) * log_g[:, None])
    g_chunk = jnp.exp(C * log_g)
    return g_chunk, decay_in, xi[:, :, None], zeta[:, :, None]


def retention_gated(proj, B, S):
    M = B * S
    tq = RET_TQ
    spb = S // tq
    g_chunk, decay_in, xi, zeta = _decay_tables()
    cos, sin = _rotary_tables(S)
    k_blk = 1
    v_blk = 2 * RET_QK_W // RET_V_W
    gate_blk = v_blk + 1
    row = lambda b, i: b * spb + i
    const3 = lambda b, i: (0, 0, 0)
    return pl.pallas_call(
        _retention_kernel,
        grid=(B, spb),
        in_specs=[
            pl.BlockSpec(memory_space=pltpu.SMEM),
            pl.BlockSpec((tq, RET_QK_W), lambda b, i: (row(b, i), 0)),
            pl.BlockSpec((tq, RET_QK_W), lambda b, i: (row(b, i), k_blk)),
            pl.BlockSpec((tq, RET_V_W), lambda b, i: (row(b, i), v_blk)),
            pl.BlockSpec((tq, RET_V_W), lambda b, i: (row(b, i), gate_blk)),
            pl.BlockSpec((tq, RET_DK), lambda b, i: (i, 0)),
            pl.BlockSpec((tq, RET_DK), lambda b, i: (i, 0)),
            pl.BlockSpec((RET_HEADS, RET_CHUNK, RET_CHUNK), const3),
            pl.BlockSpec((RET_HEADS, RET_CHUNK, 1), const3),
            pl.BlockSpec((RET_HEADS, RET_CHUNK, 1), const3),
        ],
        out_specs=pl.BlockSpec((tq, RET_V_W), lambda b, i: (row(b, i), 0)),
        out_shape=jax.ShapeDtypeStruct((M, RET_V_W), BF16),
        scratch_shapes=[pltpu.VMEM((RET_HEADS, RET_DK, RET_DV), F32)],
        compiler_params=_params(("parallel", "arbitrary")),
        name="retention",
    )(g_chunk, proj, proj, proj, proj, cos, sin, decay_in, xi, zeta)


def _mem_attn_kernel(q_ref, gate_ref, k_ref, v_ref, o_ref):
    for h in range(MEM_HEADS):
        cols = slice(h * MEM_DH, (h + 1) * MEM_DH)
        s = _dot_nt(q_ref[:, cols], k_ref[:, cols]) * (MEM_DH ** -0.5)
        p = jnp.exp(s - jnp.max(s, axis=-1, keepdims=True))
        l = jnp.sum(p, axis=-1, keepdims=True)
        o = _dot(p.astype(BF16), v_ref[:, cols]) * (1.0 / l)
        o_ref[:, cols] = (o * _silu(gate_ref[:, cols].astype(F32))).astype(o_ref.dtype)


def mem_attention_gated(proj, mem_kv, B, S, q_blk):
    M = B * S
    tq = MEM_TQ
    spb = S // tq
    row = lambda b, i: b * spb + i
    return pl.pallas_call(
        _mem_attn_kernel,
        grid=(B, spb),
        in_specs=[
            pl.BlockSpec((tq, MEM_W), lambda b, i: (row(b, i), q_blk)),
            pl.BlockSpec((tq, MEM_W), lambda b, i: (row(b, i), q_blk + 1)),
            pl.BlockSpec((MEM_LEN, MEM_W), lambda b, i: (b, 0)),
            pl.BlockSpec((MEM_LEN, MEM_W), lambda b, i: (b, 1)),
        ],
        out_specs=pl.BlockSpec((tq, MEM_W), lambda b, i: (row(b, i), 0)),
        out_shape=jax.ShapeDtypeStruct((M, MEM_W), BF16),
        compiler_params=_params(("parallel", "parallel")),
        name="mem_attention",
    )(proj, proj, mem_kv, mem_kv)


DIL_TOK = max(d for _, d in DIL_CONFIG) * DIL_BLOCK
DIL_KEYS = 2 * DIL_BLOCK


def _residue_rows(dil):
    per_class = DIL_TOK // dil
    return per_class, DIL_BLOCK + per_class


def _class_pieces(dil, r, pos0, count):
    per_perm = PERM_TOK // dil
    pieces = []
    while count:
        blk, off = divmod(pos0, per_perm)
        size = min(count, per_perm - off)
        pieces.append((blk * PERM_TOK + r * per_perm + off, size))
        pos0, count = pos0 + size, count - size
    return pieces


def _dilated_kernel(q0_ref, q1_ref, q2_ref, gate_ref, k0_ref, v0_ref, k1_ref, v1_ref, k2_ref, v2_ref, o_ref,
                    ke0_ref, ve0_ref, ke1_ref, ve1_ref, ke2_ref, ve2_ref, acc_ref, m_ref, l_ref):
    j = pl.program_id(2)
    q_refs = (q0_ref, q1_ref, q2_ref)
    kv_refs = ((k0_ref, v0_ref, ke0_ref, ve0_ref), (k1_ref, v1_ref, ke1_ref, ve1_ref), (k2_ref, v2_ref, ke2_ref, ve2_ref))

    for (k_ref, v_ref, ke_ref, ve_ref), (_, dil) in zip(kv_refs, DIL_CONFIG):
        per_class, seg = _residue_rows(dil)

        @pl.when(j == 0)
        def _zero_halo(ke_ref=ke_ref, ve_ref=ve_ref, dil=dil, seg=seg):
            for r in range(dil):
                ke_ref[r * seg:r * seg + DIL_BLOCK, :] = jnp.zeros((DIL_BLOCK, DIL_DH), BF16)
                ve_ref[r * seg:r * seg + DIL_BLOCK, :] = jnp.zeros((DIL_BLOCK, DIL_DH), BF16)

        @pl.when(j > 0)
        def _carry_halo(ke_ref=ke_ref, ve_ref=ve_ref, dil=dil, seg=seg, per_class=per_class):
            for r in range(dil):
                ke_ref[r * seg:r * seg + DIL_BLOCK, :] = ke_ref[r * seg + per_class:(r + 1) * seg, :]
                ve_ref[r * seg:r * seg + DIL_BLOCK, :] = ve_ref[r * seg + per_class:(r + 1) * seg, :]

        for r in range(dil):
            dst = r * seg + DIL_BLOCK
            for start, size in _class_pieces(dil, r, 0, per_class):
                ke_ref[dst:dst + size, :] = k_ref[start:start + size, :]
                ve_ref[dst:dst + size, :] = v_ref[start:start + size, :]
                dst += size

    qi = lax.broadcasted_iota(jnp.int32, (DIL_BLOCK, DIL_KEYS), 0)
    kk = lax.broadcasted_iota(jnp.int32, (DIL_BLOCK, DIL_KEYS), 1)
    band = (kk >= qi) & (kk <= qi + DIL_BLOCK)
    band_first = band & (kk >= jnp.where(j == 0, DIL_BLOCK, 0))
    exp2_scale = DIL_DH ** -0.5 * math.log2(math.e)

    def attend(q, ke_ref, ve_ref, key0, mask, tok, first_group, last_group):
        kc = ke_ref[pl.ds(key0, DIL_KEYS), :]
        vc = ve_ref[pl.ds(key0, DIL_KEYS), :]
        s = jnp.where(mask, _dot_nt(q, kc), -1e30)
        m_blk = jnp.max(s, axis=-1, keepdims=True)
        if first_group:
            m_new = jnp.broadcast_to(m_blk, (DIL_BLOCK, DIL_DH))
        else:
            m_old = m_ref[tok, :]
            m_new = jnp.maximum(m_old, m_blk)
            alpha = jnp.exp2((m_old - m_new) * exp2_scale)
        p = jnp.exp2((s - jnp.concatenate([m_new, m_new], axis=1)) * exp2_scale)
        l = jnp.sum(p, axis=-1, keepdims=True)
        acc = _dot(p.astype(BF16), vc)
        if not first_group:
            l = alpha * l_ref[tok, :] + l
            acc = alpha * acc_ref[tok, :] + acc
        if last_group:
            o = acc * (1.0 / l) * _silu(gate_ref[tok, :].astype(F32))
            o_ref[tok, :] = o.astype(o_ref.dtype)
        else:
            l_ref[tok, :] = jnp.broadcast_to(l, (DIL_BLOCK, DIL_DH))
            acc_ref[tok, :] = acc
            m_ref[tok, :] = m_new

    groups = sorted(range(len(DIL_CONFIG)), key=lambda g: -DIL_CONFIG[g][1])
    for group in groups:
        dil = DIL_CONFIG[group][1]
        per_class, seg = _residue_rows(dil)
        _, _, ke_ref, ve_ref = kv_refs[group]
        for r in range(dil):
            for blk in range(per_class // DIL_BLOCK):
                pieces = _class_pieces(dil, r, blk * DIL_BLOCK, DIL_BLOCK)
                q = jnp.concatenate([q_refs[group][start:start + size, :] for start, size in pieces], axis=0)
                if dil == 1:
                    tok = pl.ds(blk * DIL_BLOCK, DIL_BLOCK)
                else:
                    tok = pl.ds(blk * DIL_BLOCK * dil + r, DIL_BLOCK, stride=dil)
                attend(q, ke_ref, ve_ref, r * seg + blk * DIL_BLOCK, band_first if blk == 0 else band, tok,
                       group == groups[0], group == groups[-1])


def dilated_gated(proj, kv, B, S):
    assert S % DIL_TOK == 0
    nblk = S // DIL_TOK
    heads_per_group = DIL_W // DIL_DH

    def head_cols(group_col):
        return pl.BlockSpec((DIL_TOK, DIL_DH), lambda b, h, j: (b * nblk + j, group_col * heads_per_group + h))

    ext = lambda dil: pltpu.VMEM((dil * _residue_rows(dil)[1], DIL_DH), BF16)
    tok_f32 = pltpu.VMEM((DIL_TOK, DIL_DH), F32)
    return pl.pallas_call(
        _dilated_kernel,
        grid=(B, DIL_HEADS, nblk),
        in_specs=[head_cols(c) for c in range(4)] + [head_cols(c) for c in range(6)],
        out_specs=pl.BlockSpec((DIL_TOK, DIL_DH), lambda b, h, j: (b * nblk + j, h)),
        out_shape=jax.ShapeDtypeStruct((B * S, DIL_W), BF16),
        scratch_shapes=[ext(d) for _, d in DIL_CONFIG for _ in range(2)] + [tok_f32, tok_f32, tok_f32],
        compiler_params=_params(("arbitrary", "arbitrary", "arbitrary")),
        name="dilated_attention",
    )(proj, proj, proj, proj, kv, kv, kv, kv, kv, kv)


def kernel(x, mem, norm_a, w_in_a, w_out_a, norm_b, w_in_b, w_out_b, w_mem_kv, mem_norm_g, kv_norm_g, w_kv, final_norm_g):
    B, S, D = x.shape
    xr = x.reshape(B * S, D)
    mem_r = mem.reshape(B * mem.shape[1], D)

    w_in_a, w_out_a, w_in_b, w_out_b, w_mem_kv = (t.astype(BF16) for t in (w_in_a, w_out_a, w_in_b, w_out_b, w_mem_kv))
    w_kv = w_kv.astype(BF16)[None]
    norm_a = norm_a[:, None, :]
    norm_b = norm_b[:, None, :]
    mem_g = mem_norm_g[None, None, :]
    kv_g = kv_norm_g[None, None, :]

    dils = tuple(d for _, d in DIL_CONFIG)
    tiles_per_group = DIL_W // PROJ_TN
    group_copies = tuple(range(len(DIL_CONFIG)))
    in_b_copies = tuple(c for c in group_copies + (0,) * ((IN_W_B - len(DIL_CONFIG) * DIL_W) // DIL_W)
                        for _ in range(tiles_per_group))
    kv_copies = tuple(c for c in group_copies for _ in range(2 * tiles_per_group))

    kv = None
    for layer in range(DEPTH):
        mem_kv = norm_matmul(mem_r, mem_g, 0, w_mem_kv, layer, name=f"mem_kv_proj{layer}")
        if layer < N_A:
            proj = norm_matmul(xr, norm_a, layer, w_in_a, layer, name=f"in_proj_a{layer}")
            o_main = retention_gated(proj, B, S)
            o_mem = mem_attention_gated(proj, mem_kv, B, S, (IN_W_A - 2 * MEM_W) // MEM_W)
            xr = out_proj(o_main, o_mem, w_out_a, layer, xr, name=f"out_proj_a{layer}")
            if layer == N_A - 1:
                kv = norm_matmul(xr, kv_g, 0, w_kv, 0, name="shared_kv_proj", dils=dils, tile_copy=kv_copies)
        else:
            jb = layer - N_A
            proj = norm_matmul(xr, norm_b, jb, w_in_b, jb, name=f"in_proj_b{jb}", dils=dils, tile_copy=in_b_copies)
            o_main = dilated_gated(proj, kv, B, S)
            o_mem = mem_attention_gated(proj, mem_kv, B, S, (IN_W_B - 2 * MEM_W) // MEM_W)
            xr = out_proj(o_main, o_mem, w_out_b, jb, xr, name=f"out_proj_b{jb}")
    return rmsnorm_rows(xr, final_norm_g[None, :]).reshape(B, S, D)
```

```python
import functools
import math

import jax
import jax.numpy as jnp
from jax import lax
from jax.experimental import pallas as pl
from jax.experimental.pallas import tpu as pltpu

F32 = jnp.float32
BF16 = jnp.bfloat16

D_MODEL = 2048
DEPTH = 4
N_A = DEPTH // 2
MEM_LEN = 256
RET_HEADS = 8
RET_DK = D_MODEL // 16
RET_DV = 2 * RET_DK
RET_CHUNK = 128
RET_THETA_BASE = 10000.0
DIL_CONFIG = ((128, 1), (512, 4), (2048, 16))
DIL_HEADS = 16
DIL_DH = D_MODEL // 16
DIL_BLOCK = 128
MEM_HEADS = 4
MEM_DH = D_MODEL // 8
EPS = 1e-6

RET_QK_W = RET_HEADS * RET_DK
RET_V_W = RET_HEADS * RET_DV
DIL_W = DIL_HEADS * DIL_DH
MEM_W = MEM_HEADS * MEM_DH
IN_W_A = 2 * RET_QK_W + 2 * RET_V_W + 2 * MEM_W
IN_W_B = len(DIL_CONFIG) * DIL_W + DIL_W + 2 * MEM_W
KV_W = 2 * len(DIL_CONFIG) * DIL_W

LANES = 128
VMEM_LIMIT_BYTES = 56 * 2**20
NORM_ROW_CHUNK = 64
PROJ_TM, PROJ_TN = 1024, 1024
PROJ_TN_WIDE = 2048
OUT_TN = 1024
PERM_TOK = 256
RET_TQ = 512
MEM_TQ = 512
FINAL_TM = 512


def _params(semantics):
    return pltpu.CompilerParams(dimension_semantics=semantics, vmem_limit_bytes=VMEM_LIMIT_BYTES)


def _dot(a, b):
    return jnp.dot(a, b, preferred_element_type=F32)


def _dot_nt(a, b):
    return lax.dot_general(a, b, (((1,), (1,)), ((), ())), preferred_element_type=F32)


def _silu(x):
    return x * (1.0 / (1.0 + jnp.exp(-x)))


def _residue_order_matrix(dil):
    per_class = PERM_TOK // dil
    dst = lax.broadcasted_iota(jnp.int32, (PERM_TOK, PERM_TOK), 0)
    src = lax.broadcasted_iota(jnp.int32, (PERM_TOK, PERM_TOK), 1)
    wanted = (dst & (per_class - 1)) * dil + (dst >> int(math.log2(per_class)))
    return jnp.where(src == wanted, 1.0, 0.0).astype(BF16)


def _norm_matmul_kernel(x_ref, g_ref, w_ref, o_ref, xn_ref, *, dils, tile_copy):
    j = pl.program_id(1)

    @pl.when(j == 0)
    def _normalize():
        def body(c, carry):
            r = pl.multiple_of(c * NORM_ROW_CHUNK, NORM_ROW_CHUNK)
            xc = x_ref[pl.ds(r, NORM_ROW_CHUNK), :]
            ms = jnp.mean(xc * xc, axis=-1, keepdims=True)
            xn_ref[0, pl.ds(r, NORM_ROW_CHUNK), :] = (xc * lax.rsqrt(ms + EPS) * g_ref[...]).astype(BF16)
            return carry

        lax.fori_loop(0, x_ref.shape[0] // NORM_ROW_CHUNK, body, 0, unroll=2)
        for v in range(1, len(dils)):
            order = _residue_order_matrix(dils[v])
            for b in range(x_ref.shape[0] // PERM_TOK):
                rows = slice(b * PERM_TOK, (b + 1) * PERM_TOK)
                xn_ref[v, rows, :] = _dot(order, xn_ref[0, rows, :]).astype(BF16)

    copy = 0
    for t, v in enumerate(tile_copy):
        if v:
            copy = copy + v * (j == t).astype(jnp.int32)
    o_ref[...] = _dot(xn_ref[copy], w_ref[...]).astype(o_ref.dtype)


def norm_matmul(x, g, g_idx, w, w_idx, *, name, dils=(1,), tile_copy=None):
    M, K = x.shape
    N = w.shape[-1]
    tm, tn = min(PROJ_TM, M), min(PROJ_TN if len(dils) > 1 else PROJ_TN_WIDE, N)
    tile_copy = tile_copy or (0,) * (N // tn)
    assert dils[0] == 1 and len(tile_copy) == N // tn and tm % PERM_TOK == 0
    return pl.pallas_call(
        functools.partial(_norm_matmul_kernel, dils=dils, tile_copy=tile_copy),
        grid=(M // tm, N // tn),
        in_specs=[
            pl.BlockSpec((tm, K), lambda i, j: (i, 0)),
            pl.BlockSpec((None, 1, K), lambda i, j: (g_idx, 0, 0)),
            pl.BlockSpec((None, K, tn), lambda i, j: (w_idx, 0, j)),
        ],
        out_specs=pl.BlockSpec((tm, tn), lambda i, j: (i, j)),
        out_shape=jax.ShapeDtypeStruct((M, N), BF16),
        scratch_shapes=[pltpu.VMEM((len(dils), tm, K), BF16)],
        compiler_params=_params(("parallel", "arbitrary")),
        name=name,
    )(x, g, w)


def _out_proj_kernel(om_ref, omem_ref, w_ref, x_ref, o_ref):
    k_main = om_ref.shape[1]
    acc = _dot(om_ref[...], w_ref[:k_main, :]) + _dot(omem_ref[...], w_ref[k_main:, :])
    o_ref[...] = x_ref[...] + acc


def out_proj(o_main, o_mem, w, layer, x, *, name):
    M, k_main = o_main.shape
    k_mem = o_mem.shape[1]
    N = w.shape[-1]
    tm, tn = PROJ_TM, OUT_TN
    return pl.pallas_call(
        _out_proj_kernel,
        grid=(M // tm, N // tn),
        in_specs=[
            pl.BlockSpec((tm, k_main), lambda i, j: (i, 0)),
            pl.BlockSpec((tm, k_mem), lambda i, j: (i, 0)),
            pl.BlockSpec((None, k_main + k_mem, tn), lambda i, j: (layer, 0, j)),
            pl.BlockSpec((tm, tn), lambda i, j: (i, j)),
        ],
        out_specs=pl.BlockSpec((tm, tn), lambda i, j: (i, j)),
        out_shape=jax.ShapeDtypeStruct((M, N), F32),
        compiler_params=_params(("parallel", "parallel")),
        name=name,
    )(o_main, o_mem, w, x)


def _out_proj_norm_kernel(om_ref, omem_ref, w_ref, x_ref, g_ref, o_ref):
    k_main = om_ref.shape[1]
    o_ref[...] = x_ref[...] + (_dot(om_ref[...], w_ref[:k_main, :]) + _dot(omem_ref[...], w_ref[k_main:, :]))

    def body(c, carry):
        r = pl.multiple_of(c * NORM_ROW_CHUNK, NORM_ROW_CHUNK)
        xc = o_ref[pl.ds(r, NORM_ROW_CHUNK), :]
        ms = jnp.mean(xc * xc, axis=-1, keepdims=True)
        o_ref[pl.ds(r, NORM_ROW_CHUNK), :] = xc * lax.rsqrt(ms + EPS) * g_ref[...]
        return carry

    lax.fori_loop(0, o_ref.shape[0] // NORM_ROW_CHUNK, body, 0, unroll=2)


def out_proj_norm(o_main, o_mem, w, layer, x, g, *, name):
    M, k_main = o_main.shape
    k_mem = o_mem.shape[1]
    N = w.shape[-1]
    tm = FINAL_TM
    return pl.pallas_call(
        _out_proj_norm_kernel,
        grid=(M // tm,),
        in_specs=[
            pl.BlockSpec((tm, k_main), lambda i: (i, 0)),
            pl.BlockSpec((tm, k_mem), lambda i: (i, 0)),
            pl.BlockSpec((None, k_main + k_mem, N), lambda i: (layer, 0, 0)),
            pl.BlockSpec((tm, N), lambda i: (i, 0)),
            pl.BlockSpec((1, N), lambda i: (0, 0)),
        ],
        out_specs=pl.BlockSpec((tm, N), lambda i: (i, 0)),
        out_shape=jax.ShapeDtypeStruct((M, N), F32),
        compiler_params=_params(("parallel",)),
        name=name,
    )(o_main, o_mem, w, x, g)


def _retention_kernel(g_ref, q_ref, k_ref, v_ref, gate_ref, cos_ref, sin_ref, decay_ref, xi_ref, zeta_ref,
                      o_ref, state_ref):
    @pl.when(pl.program_id(1) == 0)
    def _reset():
        state_ref[...] = jnp.zeros_like(state_ref)

    def rotate(t, cos, sin):
        return t * cos + pltpu.roll(t, RET_DK // 2, axis=1) * sin

    for c in range(q_ref.shape[0] // RET_CHUNK):
        rows = pl.ds(c * RET_CHUNK, RET_CHUNK)
        cos = cos_ref[rows, :]
        sin = sin_ref[rows, :]
        for h in range(RET_HEADS):
            qk_cols = slice(h * RET_DK, (h + 1) * RET_DK)
            v_cols = slice(h * RET_DV, (h + 1) * RET_DV)
            q = rotate(q_ref[rows, qk_cols].astype(F32), cos, sin)
            k = rotate(k_ref[rows, qk_cols].astype(F32), cos, sin) * (RET_DK ** -0.5)
            v = v_ref[rows, v_cols]
            qb = q.astype(BF16)
            s = _dot_nt(qb, k.astype(BF16)) * decay_ref[h]
            state = state_ref[h]
            o = _dot(s.astype(BF16), v) + _dot(qb, state.astype(BF16)) * xi_ref[h]
            kz_t = (k * zeta_ref[h]).T.astype(BF16)
            state_ref[h] = g_ref[h] * state + _dot(kz_t, v)
            o = o * lax.rsqrt(jnp.mean(o * o, axis=-1, keepdims=True) + EPS)
            o_ref[rows, v_cols] = (o * _silu(gate_ref[rows, v_cols].astype(F32))).astype(o_ref.dtype)


def _rotary_tables(S):
    half = RET_DK // 2
    inv = 1.0 / (RET_THETA_BASE ** jnp.linspace(0.0, 1.0, half, dtype=F32))
    ang = jnp.arange(S, dtype=F32)[:, None] * inv[None, :]
    cos, sin = jnp.cos(ang), jnp.sin(ang)
    return jnp.concatenate([cos, cos], axis=-1), jnp.concatenate([-sin, sin], axis=-1)


def _decay_tables():
    H, C = RET_HEADS, RET_CHUNK
    log_g = jnp.log1p(-(2.0 ** (-5.0 - jnp.arange(H, dtype=F32))))
    pos = jnp.arange(C, dtype=F32)
    rel = pos[:, None] - pos[None, :]
    decay_in = jnp.where(rel >= 0, jnp.exp(jnp.maximum(rel, 0.0)[None] * log_g[:, None, None]), 0.0)
    xi = jnp.exp((pos[None, :] + 1.0) * log_g[:, None])
    zeta = jnp.exp((C - 1.0 - pos[None, :]) * log_g[:, None])
    g_chunk = jnp.exp(C * log_g)
    return g_chunk, decay_in, xi[:, :, None], zeta[:, :, None]


def retention_gated(proj, B, S):
    M = B * S
    tq = RET_TQ
    spb = S // tq
    g_chunk, decay_in, xi, zeta = _decay_tables()
    cos, sin = _rotary_tables(S)
    k_blk = 1
    v_blk = 2 * RET_QK_W // RET_V_W
    gate_blk = v_blk + 1
    row = lambda b, i: b * spb + i
    const3 = lambda b, i: (0, 0, 0)
    return pl.pallas_call(
        _retention_kernel,
        grid=(B, spb),
        in_specs=[
            pl.BlockSpec(memory_space=pltpu.SMEM),
            pl.BlockSpec((tq, RET_QK_W), lambda b, i: (row(b, i), 0)),
            pl.BlockSpec((tq, RET_QK_W), lambda b, i: (row(b, i), k_blk)),
            pl.BlockSpec((tq, RET_V_W), lambda b, i: (row(b, i), v_blk)),
            pl.BlockSpec((tq, RET_V_W), lambda b, i: (row(b, i), gate_blk)),
            pl.BlockSpec((tq, RET_DK), lambda b, i: (i, 0)),
            pl.BlockSpec((tq, RET_DK), lambda b, i: (i, 0)),
            pl.BlockSpec((RET_HEADS, RET_CHUNK, RET_CHUNK), const3),
            pl.BlockSpec((RET_HEADS, RET_CHUNK, 1), const3),
            pl.BlockSpec((RET_HEADS, RET_CHUNK, 1), const3),
        ],
        out_specs=pl.BlockSpec((tq, RET_V_W), lambda b, i: (row(b, i), 0)),
        out_shape=jax.ShapeDtypeStruct((M, RET_V_W), BF16),
        scratch_shapes=[pltpu.VMEM((RET_HEADS, RET_DK, RET_DV), F32)],
        compiler_params=_params(("parallel", "arbitrary")),
        name="retention",
    )(g_chunk, proj, proj, proj, proj, cos, sin, decay_in, xi, zeta)


def _mem_attn_kernel(q_ref, gate_ref, k_ref, v_ref, o_ref):
    for h in range(MEM_HEADS):
        cols = slice(h * MEM_DH, (h + 1) * MEM_DH)
        s = _dot_nt(q_ref[:, cols], k_ref[:, cols])
        p = jnp.exp2((s - jnp.max(s, axis=-1, keepdims=True)) * (MEM_DH ** -0.5 * math.log2(math.e)))
        l = jnp.sum(p, axis=-1, keepdims=True)
        o = _dot(p.astype(BF16), v_ref[:, cols]) * (1.0 / l)
        o_ref[:, cols] = (o * _silu(gate_ref[:, cols].astype(F32))).astype(o_ref.dtype)


def mem_attention_gated(proj, mem_kv, B, S, q_blk):
    M = B * S
    tq = MEM_TQ
    spb = S // tq
    row = lambda b, i: b * spb + i
    return pl.pallas_call(
        _mem_attn_kernel,
        grid=(B, spb),
        in_specs=[
            pl.BlockSpec((tq, MEM_W), lambda b, i: (row(b, i), q_blk)),
            pl.BlockSpec((tq, MEM_W), lambda b, i: (row(b, i), q_blk + 1)),
            pl.BlockSpec((MEM_LEN, MEM_W), lambda b, i: (b, 0)),
            pl.BlockSpec((MEM_LEN, MEM_W), lambda b, i: (b, 1)),
        ],
        out_specs=pl.BlockSpec((tq, MEM_W), lambda b, i: (row(b, i), 0)),
        out_shape=jax.ShapeDtypeStruct((M, MEM_W), BF16),
        compiler_params=_params(("parallel", "parallel")),
        name="mem_attention",
    )(proj, proj, mem_kv, mem_kv)


DIL_TOK = max(d for _, d in DIL_CONFIG) * DIL_BLOCK
DIL_KEYS = 2 * DIL_BLOCK


def _residue_rows(dil):
    per_class = DIL_TOK // dil
    return per_class, DIL_BLOCK + per_class


def _class_pieces(dil, r, pos0, count):
    per_perm = PERM_TOK // dil
    pieces = []
    while count:
        blk, off = divmod(pos0, per_perm)
        size = min(count, per_perm - off)
        pieces.append((blk * PERM_TOK + r * per_perm + off, size))
        pos0, count = pos0 + size, count - size
    return pieces


def _dilated_kernel(q0_ref, q1_ref, q2_ref, gate_ref, k0_ref, v0_ref, k1_ref, v1_ref, k2_ref, v2_ref, o_ref,
                    ke0_ref, ve0_ref, ke1_ref, ve1_ref, ke2_ref, ve2_ref, acc_ref, m_ref, l_ref):
    j = pl.program_id(2)
    q_refs = (q0_ref, q1_ref, q2_ref)
    kv_refs = ((k0_ref, v0_ref, ke0_ref, ve0_ref), (k1_ref, v1_ref, ke1_ref, ve1_ref), (k2_ref, v2_ref, ke2_ref, ve2_ref))

    for (k_ref, v_ref, ke_ref, ve_ref), (_, dil) in zip(kv_refs, DIL_CONFIG):
        per_class, seg = _residue_rows(dil)

        @pl.when(j == 0)
        def _zero_halo(ke_ref=ke_ref, ve_ref=ve_ref, dil=dil, seg=seg):
            for r in range(dil):
                ke_ref[r * seg:r * seg + DIL_BLOCK, :] = jnp.zeros((DIL_BLOCK, DIL_DH), BF16)
                ve_ref[r * seg:r * seg + DIL_BLOCK, :] = jnp.zeros((DIL_BLOCK, DIL_DH), BF16)

        @pl.when(j > 0)
        def _carry_halo(ke_ref=ke_ref, ve_ref=ve_ref, dil=dil, seg=seg, per_class=per_class):
            for r in range(dil):
                ke_ref[r * seg:r * seg + DIL_BLOCK, :] = ke_ref[r * seg + per_class:(r + 1) * seg, :]
                ve_ref[r * seg:r * seg + DIL_BLOCK, :] = ve_ref[r * seg + per_class:(r + 1) * seg, :]

        for r in range(dil):
            dst = r * seg + DIL_BLOCK
            for start, size in _class_pieces(dil, r, 0, per_class):
                ke_ref[dst:dst + size, :] = k_ref[start:start + size, :]
                ve_ref[dst:dst + size, :] = v_ref[start:start + size, :]
                dst += size

    key_lead = (lax.broadcasted_iota(jnp.int32, (DIL_BLOCK, DIL_BLOCK), 1)
                - lax.broadcasted_iota(jnp.int32, (DIL_BLOCK, DIL_BLOCK), 0))
    first_min_lead = jnp.where(j == 0, DIL_BLOCK, 0)
    exp2_scale = DIL_DH ** -0.5 * math.log2(math.e)

    def attend(q, ke_ref, ve_ref, key0, prev_min_lead, tok, first_group, last_group):
        kc = ke_ref[pl.ds(key0, DIL_KEYS), :]
        vc = ve_ref[pl.ds(key0, DIL_KEYS), :]
        s = _dot_nt(q, kc)
        s_prev = jnp.where(key_lead >= prev_min_lead, s[:, :DIL_BLOCK], -1e30)
        s_cur = jnp.where(key_lead <= 0, s[:, DIL_BLOCK:], -1e30)
        m_blk = jnp.max(jnp.maximum(s_prev, s_cur), axis=-1, keepdims=True)
        if first_group:
            m_new = jnp.broadcast_to(m_blk, (DIL_BLOCK, DIL_DH))
        else:
            m_old = m_ref[tok, :]
            m_new = jnp.maximum(m_old, m_blk)
            alpha = jnp.exp2((m_old - m_new) * exp2_scale)
        p_prev = jnp.exp2((s_prev - m_new) * exp2_scale)
        p_cur = jnp.exp2((s_cur - m_new) * exp2_scale)
        l = jnp.sum(p_prev + p_cur, axis=-1, keepdims=True)
        acc = _dot(jnp.concatenate([p_prev, p_cur], axis=1).astype(BF16), vc)
        if not first_group:
            l = alpha * l_ref[tok, :] + l
            acc = alpha * acc_ref[tok, :] + acc
        if last_group:
            o = acc * (1.0 / l) * _silu(gate_ref[tok, :].astype(F32))
            o_ref[tok, :] = o.astype(o_ref.dtype)
        else:
            l_ref[tok, :] = jnp.broadcast_to(l, (DIL_BLOCK, DIL_DH))
            acc_ref[tok, :] = acc
            m_ref[tok, :] = m_new

    groups = sorted(range(len(DIL_CONFIG)), key=lambda g: -DIL_CONFIG[g][1])
    for group in groups:
        dil = DIL_CONFIG[group][1]
        per_class, seg = _residue_rows(dil)
        _, _, ke_ref, ve_ref = kv_refs[group]
        for r in range(dil):
            for blk in range(per_class // DIL_BLOCK):
                pieces = _class_pieces(dil, r, blk * DIL_BLOCK, DIL_BLOCK)
                q = jnp.concatenate([q_refs[group][start:start + size, :] for start, size in pieces], axis=0)
                if dil == 1:
                    tok = pl.ds(blk * DIL_BLOCK, DIL_BLOCK)
                else:
                    tok = pl.ds(blk * DIL_BLOCK * dil + r, DIL_BLOCK, stride=dil)
                attend(q, ke_ref, ve_ref, r * seg + blk * DIL_BLOCK, first_min_lead if blk == 0 else 0, tok,
                       group == groups[0], group == groups[-1])


def dilated_gated(proj, kv, B, S):
    assert S % DIL_TOK == 0
    nblk = S // DIL_TOK
    heads_per_group = DIL_W // DIL_DH

    def head_cols(group_col):
        return pl.BlockSpec((DIL_TOK, DIL_DH), lambda b, h, j: (b * nblk + j, group_col * heads_per_group + h))

    ext = lambda dil: pltpu.VMEM((dil * _residue_rows(dil)[1], DIL_DH), BF16)
    tok_f32 = pltpu.VMEM((DIL_TOK, DIL_DH), F32)
    return pl.pallas_call(
        _dilated_kernel,
        grid=(B, DIL_HEADS, nblk),
        in_specs=[head_cols(c) for c in range(4)] + [head_cols(c) for c in range(6)],
        out_specs=pl.BlockSpec((DIL_TOK, DIL_DH), lambda b, h, j: (b * nblk + j, h)),
        out_shape=jax.ShapeDtypeStruct((B * S, DIL_W), BF16),
        scratch_shapes=[ext(d) for _, d in DIL_CONFIG for _ in range(2)] + [tok_f32, tok_f32, tok_f32],
        compiler_params=_params(("arbitrary", "arbitrary", "arbitrary")),
        name="dilated_attention",
    )(proj, proj, proj, proj, kv, kv, kv, kv, kv, kv)


def kernel(x, mem, norm_a, w_in_a, w_out_a, norm_b, w_in_b, w_out_b, w_mem_kv, mem_norm_g, kv_norm_g, w_kv, final_norm_g):
    B, S, D = x.shape
    xr = x.reshape(B * S, D)
    mem_r = mem.reshape(B * mem.shape[1], D)

    w_in_a, w_out_a, w_in_b, w_out_b, w_mem_kv = (t.astype(BF16) for t in (w_in_a, w_out_a, w_in_b, w_out_b, w_mem_kv))
    w_kv = w_kv.astype(BF16)[None]
    norm_a = norm_a[:, None, :]
    norm_b = norm_b[:, None, :]
    mem_g = mem_norm_g[None, None, :]
    kv_g = kv_norm_g[None, None, :]

    dils = tuple(d for _, d in DIL_CONFIG)
    tiles_per_group = DIL_W // PROJ_TN
    group_copies = tuple(range(len(DIL_CONFIG)))
    in_b_copies = tuple(c for c in group_copies + (0,) * ((IN_W_B - len(DIL_CONFIG) * DIL_W) // DIL_W)
                        for _ in range(tiles_per_group))
    kv_copies = tuple(c for c in group_copies for _ in range(2 * tiles_per_group))

    kv = None
    for layer in range(DEPTH):
        mem_kv = norm_matmul(mem_r, mem_g, 0, w_mem_kv, layer, name=f"mem_kv_proj{layer}")
        if layer < N_A:
            proj = norm_matmul(xr, norm_a, layer, w_in_a, layer, name=f"in_proj_a{layer}")
            o_main = retention_gated(proj, B, S)
            o_mem = mem_attention_gated(proj, mem_kv, B, S, (IN_W_A - 2 * MEM_W) // MEM_W)
            xr = out_proj(o_main, o_mem, w_out_a, layer, xr, name=f"out_proj_a{layer}")
            if layer == N_A - 1:
                kv = norm_matmul(xr, kv_g, 0, w_kv, 0, name="shared_kv_proj", dils=dils, tile_copy=kv_copies)
        else:
            jb = layer - N_A
            proj = norm_matmul(xr, norm_b, jb, w_in_b, jb, name=f"in_proj_b{jb}", dils=dils, tile_copy=in_b_copies)
            o_main = dilated_gated(proj, kv, B, S)
            o_mem = mem_attention_gated(proj, mem_kv, B, S, (IN_W_B - 2 * MEM_W) // MEM_W)
            if layer < DEPTH - 1:
                xr = out_proj(o_main, o_mem, w_out_b, jb, xr, name=f"out_proj_b{jb}")
            else:
                xr = out_proj_norm(o_main, o_mem, w_out_b, jb, xr, final_norm_g[None, :], name="out_proj_final_norm")
    return xr.reshape(B, S, D)
```

```python
import functools
import math

import jax
import jax.numpy as jnp
from jax import lax
from jax.experimental import pallas as pl
from jax.experimental.pallas import tpu as pltpu

F32 = jnp.float32
BF16 = jnp.bfloat16

D_MODEL = 2048
DEPTH = 4
N_A = DEPTH // 2
MEM_LEN = 256
RET_HEADS = 8
RET_DK = D_MODEL // 16
RET_DV = 2 * RET_DK
RET_CHUNK = 128
RET_THETA_BASE = 10000.0
DIL_CONFIG = ((128, 1), (512, 4), (2048, 16))
DIL_HEADS = 16
DIL_DH = D_MODEL // 16
DIL_BLOCK = 128
MEM_HEADS = 4
MEM_DH = D_MODEL // 8
EPS = 1e-6

RET_QK_W = RET_HEADS * RET_DK
RET_V_W = RET_HEADS * RET_DV
DIL_W = DIL_HEADS * DIL_DH
MEM_W = MEM_HEADS * MEM_DH
IN_W_A = 2 * RET_QK_W + 2 * RET_V_W + 2 * MEM_W
IN_W_B = len(DIL_CONFIG) * DIL_W + DIL_W + 2 * MEM_W
KV_W = 2 * len(DIL_CONFIG) * DIL_W

LANES = 128
VMEM_LIMIT_BYTES = 62 * 2**20
NORM_ROW_CHUNK = 64
PROJ_TM, PROJ_TN = 1024, 2048
OUT_TN = 1024
PERM_TOK = 256
RET_TQ = 512
MEM_TQ = 512
FINAL_TM = 512


def _params(semantics):
    return pltpu.CompilerParams(dimension_semantics=semantics, vmem_limit_bytes=VMEM_LIMIT_BYTES)


def _dot(a, b):
    return jnp.dot(a, b, preferred_element_type=F32)


def _dot_nt(a, b):
    return lax.dot_general(a, b, (((1,), (1,)), ((), ())), preferred_element_type=F32)


def _silu(x):
    return x * (1.0 / (1.0 + jnp.exp(-x)))


def _residue_order_matrix(dil):
    per_class = PERM_TOK // dil
    dst = lax.broadcasted_iota(jnp.int32, (PERM_TOK, PERM_TOK), 0)
    src = lax.broadcasted_iota(jnp.int32, (PERM_TOK, PERM_TOK), 1)
    wanted = (dst & (per_class - 1)) * dil + (dst >> int(math.log2(per_class)))
    return jnp.where(src == wanted, 1.0, 0.0).astype(BF16)


def _norm_matmul_kernel(x_ref, g_ref, w_ref, o_ref, xn_ref, *, dils, tile_copy):
    j = pl.program_id(1)

    @pl.when(j == 0)
    def _normalize():
        def body(c, carry):
            r = pl.multiple_of(c * NORM_ROW_CHUNK, NORM_ROW_CHUNK)
            xc = x_ref[pl.ds(r, NORM_ROW_CHUNK), :]
            ms = jnp.mean(xc * xc, axis=-1, keepdims=True)
            xn_ref[0, pl.ds(r, NORM_ROW_CHUNK), :] = (xc * lax.rsqrt(ms + EPS) * g_ref[...]).astype(BF16)
            return carry

        lax.fori_loop(0, x_ref.shape[0] // NORM_ROW_CHUNK, body, 0, unroll=2)
        for v in range(1, len(dils)):
            order = _residue_order_matrix(dils[v])
            for b in range(x_ref.shape[0] // PERM_TOK):
                rows = slice(b * PERM_TOK, (b + 1) * PERM_TOK)
                xn_ref[v, rows, :] = _dot(order, xn_ref[0, rows, :]).astype(BF16)

    copy = 0
    for t, v in enumerate(tile_copy):
        if v:
            copy = copy + v * (j == t).astype(jnp.int32)
    o_ref[...] = _dot(xn_ref[copy], w_ref[...]).astype(o_ref.dtype)


def norm_matmul(x, g, g_idx, w, w_idx, *, name, dils=(1,), tile_copy=None):
    M, K = x.shape
    N = w.shape[-1]
    tm, tn = min(PROJ_TM, M), min(PROJ_TN, N)
    tile_copy = tile_copy or (0,) * (N // tn)
    assert dils[0] == 1 and len(tile_copy) == N // tn and tm % PERM_TOK == 0
    return pl.pallas_call(
        functools.partial(_norm_matmul_kernel, dils=dils, tile_copy=tile_copy),
        grid=(M // tm, N // tn),
        in_specs=[
            pl.BlockSpec((tm, K), lambda i, j: (i, 0)),
            pl.BlockSpec((None, 1, K), lambda i, j: (g_idx, 0, 0)),
            pl.BlockSpec((None, K, tn), lambda i, j: (w_idx, 0, j)),
        ],
        out_specs=pl.BlockSpec((tm, tn), lambda i, j: (i, j)),
        out_shape=jax.ShapeDtypeStruct((M, N), BF16),
        scratch_shapes=[pltpu.VMEM((len(dils), tm, K), BF16)],
        compiler_params=_params(("parallel", "arbitrary")),
        name=name,
    )(x, g, w)


def _out_proj_kernel(om_ref, omem_ref, w_ref, x_ref, o_ref):
    k_main = om_ref.shape[1]
    acc = _dot(om_ref[...], w_ref[:k_main, :]) + _dot(omem_ref[...], w_ref[k_main:, :])
    o_ref[...] = x_ref[...] + acc


def out_proj(o_main, o_mem, w, layer, x, *, name):
    M, k_main = o_main.shape
    k_mem = o_mem.shape[1]
    N = w.shape[-1]
    tm, tn = PROJ_TM, OUT_TN
    return pl.pallas_call(
        _out_proj_kernel,
        grid=(M // tm, N // tn),
        in_specs=[
            pl.BlockSpec((tm, k_main), lambda i, j: (i, 0)),
            pl.BlockSpec((tm, k_mem), lambda i, j: (i, 0)),
            pl.BlockSpec((None, k_main + k_mem, tn), lambda i, j: (layer, 0, j)),
            pl.BlockSpec((tm, tn), lambda i, j: (i, j)),
        ],
        out_specs=pl.BlockSpec((tm, tn), lambda i, j: (i, j)),
        out_shape=jax.ShapeDtypeStruct((M, N), F32),
        compiler_params=_params(("parallel", "parallel")),
        name=name,
    )(o_main, o_mem, w, x)


def _out_proj_norm_kernel(om_ref, omem_ref, w_ref, x_ref, g_ref, o_ref):
    k_main = om_ref.shape[1]
    o_ref[...] = x_ref[...] + (_dot(om_ref[...], w_ref[:k_main, :]) + _dot(omem_ref[...], w_ref[k_main:, :]))

    def body(c, carry):
        r = pl.multiple_of(c * NORM_ROW_CHUNK, NORM_ROW_CHUNK)
        xc = o_ref[pl.ds(r, NORM_ROW_CHUNK), :]
        ms = jnp.mean(xc * xc, axis=-1, keepdims=True)
        o_ref[pl.ds(r, NORM_ROW_CHUNK), :] = xc * lax.rsqrt(ms + EPS) * g_ref[...]
        return carry

    lax.fori_loop(0, o_ref.shape[0] // NORM_ROW_CHUNK, body, 0, unroll=2)


def out_proj_norm(o_main, o_mem, w, layer, x, g, *, name):
    M, k_main = o_main.shape
    k_mem = o_mem.shape[1]
    N = w.shape[-1]
    tm = FINAL_TM
    return pl.pallas_call(
        _out_proj_norm_kernel,
        grid=(M // tm,),
        in_specs=[
            pl.BlockSpec((tm, k_main), lambda i: (i, 0)),
            pl.BlockSpec((tm, k_mem), lambda i: (i, 0)),
            pl.BlockSpec((None, k_main + k_mem, N), lambda i: (layer, 0, 0)),
            pl.BlockSpec((tm, N), lambda i: (i, 0)),
            pl.BlockSpec((1, N), lambda i: (0, 0)),
        ],
        out_specs=pl.BlockSpec((tm, N), lambda i: (i, 0)),
        out_shape=jax.ShapeDtypeStruct((M, N), F32),
        compiler_params=_params(("parallel",)),
        name=name,
    )(o_main, o_mem, w, x, g)


def _retention_kernel(g_ref, q_ref, k_ref, v_ref, gate_ref, cos_ref, sin_ref, decay_ref, xi_ref, zeta_ref,
                      o_ref, state_ref):
    @pl.when(pl.program_id(1) == 0)
    def _reset():
        state_ref[...] = jnp.zeros_like(state_ref)

    def rotate(t, cos, sin):
        return t * cos + pltpu.roll(t, RET_DK // 2, axis=1) * sin

    for c in range(q_ref.shape[0] // RET_CHUNK):
        rows = pl.ds(c * RET_CHUNK, RET_CHUNK)
        cos = cos_ref[rows, :]
        sin = sin_ref[rows, :]
        for h in range(RET_HEADS):
            qk_cols = slice(h * RET_DK, (h + 1) * RET_DK)
            v_cols = slice(h * RET_DV, (h + 1) * RET_DV)
            q = rotate(q_ref[rows, qk_cols].astype(F32), cos, sin)
            k = rotate(k_ref[rows, qk_cols].astype(F32), cos, sin) * (RET_DK ** -0.5)
            v = v_ref[rows, v_cols]
            qb = q.astype(BF16)
            s = _dot_nt(qb, k.astype(BF16)) * decay_ref[h]
            state = state_ref[h]
            o = _dot(s.astype(BF16), v) + _dot(qb, state.astype(BF16)) * xi_ref[h]
            kz_t = (k * zeta_ref[h]).T.astype(BF16)
            state_ref[h] = g_ref[h] * state + _dot(kz_t, v)
            o = o * lax.rsqrt(jnp.mean(o * o, axis=-1, keepdims=True) + EPS)
            o_ref[rows, v_cols] = (o * _silu(gate_ref[rows, v_cols].astype(F32))).astype(o_ref.dtype)


def _rotary_tables(S):
    half = RET_DK // 2
    inv = 1.0 / (RET_THETA_BASE ** jnp.linspace(0.0, 1.0, half, dtype=F32))
    ang = jnp.arange(S, dtype=F32)[:, None] * inv[None, :]
    cos, sin = jnp.cos(ang), jnp.sin(ang)
    return jnp.concatenate([cos, cos], axis=-1), jnp.concatenate([-sin, sin], axis=-1)


def _decay_tables():
    H, C = RET_HEADS, RET_CHUNK
    log_g = jnp.log1p(-(2.0 ** (-5.0 - jnp.arange(H, dtype=F32))))
    pos = jnp.arange(C, dtype=F32)
    rel = pos[:, None] - pos[None, :]
    decay_in = jnp.where(rel >= 0, jnp.exp(jnp.maximum(rel, 0.0)[None] * log_g[:, None, None]), 0.0)
    xi = jnp.exp((pos[None, :] + 1.0) * log_g[:, None])
    zeta = jnp.exp((C - 1.0 - pos[None, :]) * log_g[:, None])
    g_chunk = jnp.exp(C * log_g)
    return g_chunk, decay_in, xi[:, :, None], zeta[:, :, None]


def retention_gated(proj, B, S):
    M = B * S
    tq = RET_TQ
    spb = S // tq
    g_chunk, decay_in, xi, zeta = _decay_tables()
    cos, sin = _rotary_tables(S)
    k_blk = 1
    v_blk = 2 * RET_QK_W // RET_V_W
    gate_blk = v_blk + 1
    row = lambda b, i: b * spb + i
    const3 = lambda b, i: (0, 0, 0)
    return pl.pallas_call(
        _retention_kernel,
        grid=(B, spb),
        in_specs=[
            pl.BlockSpec(memory_space=pltpu.SMEM),
            pl.BlockSpec((tq, RET_QK_W), lambda b, i: (row(b, i), 0)),
            pl.BlockSpec((tq, RET_QK_W), lambda b, i: (row(b, i), k_blk)),
            pl.BlockSpec((tq, RET_V_W), lambda b, i: (row(b, i), v_blk)),
            pl.BlockSpec((tq, RET_V_W), lambda b, i: (row(b, i), gate_blk)),
            pl.BlockSpec((tq, RET_DK), lambda b, i: (i, 0)),
            pl.BlockSpec((tq, RET_DK), lambda b, i: (i, 0)),
            pl.BlockSpec((RET_HEADS, RET_CHUNK, RET_CHUNK), const3),
            pl.BlockSpec((RET_HEADS, RET_CHUNK, 1), const3),
            pl.BlockSpec((RET_HEADS, RET_CHUNK, 1), const3),
        ],
        out_specs=pl.BlockSpec((tq, RET_V_W), lambda b, i: (row(b, i), 0)),
        out_shape=jax.ShapeDtypeStruct((M, RET_V_W), BF16),
        scratch_shapes=[pltpu.VMEM((RET_HEADS, RET_DK, RET_DV), F32)],
        compiler_params=_params(("parallel", "arbitrary")),
        name="retention",
    )(g_chunk, proj, proj, proj, proj, cos, sin, decay_in, xi, zeta)


def _mem_attn_kernel(q_ref, gate_ref, k_ref, v_ref, o_ref):
    for h in range(MEM_HEADS):
        cols = slice(h * MEM_DH, (h + 1) * MEM_DH)
        s = _dot_nt(q_ref[:, cols], k_ref[:, cols])
        p = jnp.exp2((s - jnp.max(s, axis=-1, keepdims=True)) * (MEM_DH ** -0.5 * math.log2(math.e)))
        l = jnp.sum(p, axis=-1, keepdims=True)
        o = _dot(p.astype(BF16), v_ref[:, cols]) * (1.0 / l)
        o_ref[:, cols] = (o * _silu(gate_ref[:, cols].astype(F32))).astype(o_ref.dtype)


def mem_attention_gated(proj, mem_kv, B, S, q_blk):
    M = B * S
    tq = MEM_TQ
    spb = S // tq
    row = lambda b, i: b * spb + i
    return pl.pallas_call(
        _mem_attn_kernel,
        grid=(B, spb),
        in_specs=[
            pl.BlockSpec((tq, MEM_W), lambda b, i: (row(b, i), q_blk)),
            pl.BlockSpec((tq, MEM_W), lambda b, i: (row(b, i), q_blk + 1)),
            pl.BlockSpec((MEM_LEN, MEM_W), lambda b, i: (b, 0)),
            pl.BlockSpec((MEM_LEN, MEM_W), lambda b, i: (b, 1)),
        ],
        out_specs=pl.BlockSpec((tq, MEM_W), lambda b, i: (row(b, i), 0)),
        out_shape=jax.ShapeDtypeStruct((M, MEM_W), BF16),
        compiler_params=_params(("parallel", "parallel")),
        name="mem_attention",
    )(proj, proj, mem_kv, mem_kv)


DIL_TOK = max(d for _, d in DIL_CONFIG) * DIL_BLOCK
DIL_KEYS = 2 * DIL_BLOCK


def _residue_rows(dil):
    per_class = DIL_TOK // dil
    return per_class, DIL_BLOCK + per_class


def _class_pieces(dil, r, pos0, count):
    per_perm = PERM_TOK // dil
    pieces = []
    while count:
        blk, off = divmod(pos0, per_perm)
        size = min(count, per_perm - off)
        pieces.append((blk * PERM_TOK + r * per_perm + off, size))
        pos0, count = pos0 + size, count - size
    return pieces


def _dilated_kernel(q0_ref, q1_ref, q2_ref, gate_ref, k0_ref, v0_ref, k1_ref, v1_ref, k2_ref, v2_ref, o_ref,
                    ke0_ref, ve0_ref, ke1_ref, ve1_ref, ke2_ref, ve2_ref, acc_ref, m_ref, l_ref):
    j = pl.program_id(2)
    q_refs = (q0_ref, q1_ref, q2_ref)
    kv_refs = ((k0_ref, v0_ref, ke0_ref, ve0_ref), (k1_ref, v1_ref, ke1_ref, ve1_ref), (k2_ref, v2_ref, ke2_ref, ve2_ref))

    for (k_ref, v_ref, ke_ref, ve_ref), (_, dil) in zip(kv_refs, DIL_CONFIG):
        per_class, seg = _residue_rows(dil)

        @pl.when(j == 0)
        def _zero_halo(ke_ref=ke_ref, ve_ref=ve_ref, dil=dil, seg=seg):
            for r in range(dil):
                ke_ref[r * seg:r * seg + DIL_BLOCK, :] = jnp.zeros((DIL_BLOCK, DIL_DH), BF16)
                ve_ref[r * seg:r * seg + DIL_BLOCK, :] = jnp.zeros((DIL_BLOCK, DIL_DH), BF16)

        @pl.when(j > 0)
        def _carry_halo(ke_ref=ke_ref, ve_ref=ve_ref, dil=dil, seg=seg, per_class=per_class):
            for r in range(dil):
                ke_ref[r * seg:r * seg + DIL_BLOCK, :] = ke_ref[r * seg + per_class:(r + 1) * seg, :]
                ve_ref[r * seg:r * seg + DIL_BLOCK, :] = ve_ref[r * seg + per_class:(r + 1) * seg, :]

        for r in range(dil):
            dst = r * seg + DIL_BLOCK
            for start, size in _class_pieces(dil, r, 0, per_class):
                ke_ref[dst:dst + size, :] = k_ref[start:start + size, :]
                ve_ref[dst:dst + size, :] = v_ref[start:start + size, :]
                dst += size

    key_lead = (lax.broadcasted_iota(jnp.int32, (DIL_BLOCK, DIL_BLOCK), 1)
                - lax.broadcasted_iota(jnp.int32, (DIL_BLOCK, DIL_BLOCK), 0))
    first_min_lead = jnp.where(j == 0, DIL_BLOCK, 0)
    exp2_scale = DIL_DH ** -0.5 * math.log2(math.e)

    def attend(q, ke_ref, ve_ref, key0, prev_min_lead, tok, first_group, last_group):
        kc = ke_ref[pl.ds(key0, DIL_KEYS), :]
        vc = ve_ref[pl.ds(key0, DIL_KEYS), :]
        s = _dot_nt(q, kc)
        s_prev = jnp.where(key_lead >= prev_min_lead, s[:, :DIL_BLOCK], -1e30)
        s_cur = jnp.where(key_lead <= 0, s[:, DIL_BLOCK:], -1e30)
        m_blk = jnp.max(jnp.maximum(s_prev, s_cur), axis=-1, keepdims=True)
        if first_group:
            m_new = jnp.broadcast_to(m_blk, (DIL_BLOCK, DIL_DH))
        else:
            m_old = m_ref[tok, :]
            m_new = jnp.maximum(m_old, m_blk)
            alpha = jnp.exp2((m_old - m_new) * exp2_scale)
        p_prev = jnp.exp2((s_prev - m_new) * exp2_scale)
        p_cur = jnp.exp2((s_cur - m_new) * exp2_scale)
        l = jnp.sum(p_prev + p_cur, axis=-1, keepdims=True)
        acc = _dot(jnp.concatenate([p_prev, p_cur], axis=1).astype(BF16), vc)
        if not first_group:
            l = alpha * l_ref[tok, :] + l
            acc = alpha * acc_ref[tok, :] + acc
        if last_group:
            o = acc * (1.0 / l) * _silu(gate_ref[tok, :].astype(F32))
            o_ref[tok, :] = o.astype(o_ref.dtype)
        else:
            l_ref[tok, :] = jnp.broadcast_to(l, (DIL_BLOCK, DIL_DH))
            acc_ref[tok, :] = acc
            m_ref[tok, :] = m_new

    groups = sorted(range(len(DIL_CONFIG)), key=lambda g: -DIL_CONFIG[g][1])
    for group in groups:
        dil = DIL_CONFIG[group][1]
        per_class, seg = _residue_rows(dil)
        _, _, ke_ref, ve_ref = kv_refs[group]
        for r in range(dil):
            for blk in range(per_class // DIL_BLOCK):
                pieces = _class_pieces(dil, r, blk * DIL_BLOCK, DIL_BLOCK)
                q = jnp.concatenate([q_refs[group][start:start + size, :] for start, size in pieces], axis=0)
                if dil == 1:
                    tok = pl.ds(blk * DIL_BLOCK, DIL_BLOCK)
                else:
                    tok = pl.ds(blk * DIL_BLOCK * dil + r, DIL_BLOCK, stride=dil)
                attend(q, ke_ref, ve_ref, r * seg + blk * DIL_BLOCK, first_min_lead if blk == 0 else 0, tok,
                       group == groups[0], group == groups[-1])


def dilated_gated(proj, kv, B, S):
    assert S % DIL_TOK == 0
    nblk = S // DIL_TOK
    heads_per_group = DIL_W // DIL_DH

    def head_cols(group_col):
        return pl.BlockSpec((DIL_TOK, DIL_DH), lambda b, h, j: (b * nblk + j, group_col * heads_per_group + h))

    ext = lambda dil: pltpu.VMEM((dil * _residue_rows(dil)[1], DIL_DH), BF16)
    tok_f32 = pltpu.VMEM((DIL_TOK, DIL_DH), F32)
    return pl.pallas_call(
        _dilated_kernel,
        grid=(B, DIL_HEADS, nblk),
        in_specs=[head_cols(c) for c in range(4)] + [head_cols(c) for c in range(6)],
        out_specs=pl.BlockSpec((DIL_TOK, DIL_DH), lambda b, h, j: (b * nblk + j, h)),
        out_shape=jax.ShapeDtypeStruct((B * S, DIL_W), BF16),
        scratch_shapes=[ext(d) for _, d in DIL_CONFIG for _ in range(2)] + [tok_f32, tok_f32, tok_f32],
        compiler_params=_params(("arbitrary", "arbitrary", "arbitrary")),
        name="dilated_attention",
    )(proj, proj, proj, proj, kv, kv, kv, kv, kv, kv)


def kernel(x, mem, norm_a, w_in_a, w_out_a, norm_b, w_in_b, w_out_b, w_mem_kv, mem_norm_g, kv_norm_g, w_kv, final_norm_g):
    B, S, D = x.shape
    xr = x.reshape(B * S, D)
    mem_r = mem.reshape(B * mem.shape[1], D)

    w_in_a, w_out_a, w_in_b, w_out_b, w_mem_kv = (t.astype(BF16) for t in (w_in_a, w_out_a, w_in_b, w_out_b, w_mem_kv))
    w_kv = w_kv.astype(BF16)[None]
    norm_a = norm_a[:, None, :]
    norm_b = norm_b[:, None, :]
    mem_g = mem_norm_g[None, None, :]
    kv_g = kv_norm_g[None, None, :]

    dils = tuple(d for _, d in DIL_CONFIG)
    tiles_per_group = DIL_W // PROJ_TN
    group_copies = tuple(range(len(DIL_CONFIG)))
    in_b_copies = tuple(c for c in group_copies + (0,) * ((IN_W_B - len(DIL_CONFIG) * DIL_W) // DIL_W)
                        for _ in range(tiles_per_group))
    kv_copies = tuple(c for c in group_copies for _ in range(2 * tiles_per_group))

    kv = None
    for layer in range(DEPTH):
        mem_kv = norm_matmul(mem_r, mem_g, 0, w_mem_kv, layer, name=f"mem_kv_proj{layer}")
        if layer < N_A:
            proj = norm_matmul(xr, norm_a, layer, w_in_a, layer, name=f"in_proj_a{layer}")
            o_main = retention_gated(proj, B, S)
            o_mem = mem_attention_gated(proj, mem_kv, B, S, (IN_W_A - 2 * MEM_W) // MEM_W)
            xr = out_proj(o_main, o_mem, w_out_a, layer, xr, name=f"out_proj_a{layer}")
            if layer == N_A - 1:
                kv = norm_matmul(xr, kv_g, 0, w_kv, 0, name="shared_kv_proj", dils=dils, tile_copy=kv_copies)
        else:
            jb = layer - N_A
            proj = norm_matmul(xr, norm_b, jb, w_in_b, jb, name=f"in_proj_b{jb}", dils=dils, tile_copy=in_b_copies)
            o_main = dilated_gated(proj, kv, B, S)
            o_mem = mem_attention_gated(proj, mem_kv, B, S, (IN_W_B - 2 * MEM_W) // MEM_W)
            if layer < DEPTH - 1:
                xr = out_proj(o_main, o_mem, w_out_b, jb, xr, name=f"out_proj_b{jb}")
            else:
                xr = out_proj_norm(o_main, o_mem, w_out_b, jb, xr, final_norm_g[None, :], name="out_proj_final_norm")
    return xr.reshape(B, S, D)
```

```python
import functools
import math

import jax
import jax.numpy as jnp
from jax import lax
from jax.experimental import pallas as pl
from jax.experimental.pallas import tpu as pltpu

F32 = jnp.float32
BF16 = jnp.bfloat16

D_MODEL = 2048
DEPTH = 4
N_A = DEPTH // 2
MEM_LEN = 256
RET_HEADS = 8
RET_DK = D_MODEL // 16
RET_DV = 2 * RET_DK
RET_CHUNK = 128
RET_THETA_BASE = 10000.0
DIL_CONFIG = ((128, 1), (512, 4), (2048, 16))
DIL_HEADS = 16
DIL_DH = D_MODEL // 16
DIL_BLOCK = 128
MEM_HEADS = 4
MEM_DH = D_MODEL // 8
EPS = 1e-6

RET_QK_W = RET_HEADS * RET_DK
RET_V_W = RET_HEADS * RET_DV
DIL_W = DIL_HEADS * DIL_DH
MEM_W = MEM_HEADS * MEM_DH
IN_W_A = 2 * RET_QK_W + 2 * RET_V_W + 2 * MEM_W
IN_W_B = len(DIL_CONFIG) * DIL_W + DIL_W + 2 * MEM_W
KV_W = 2 * len(DIL_CONFIG) * DIL_W

VMEM_LIMIT_BYTES = 62 * 2**20
NORM_ROW_CHUNK = 64
PROJ_TM, PROJ_TN = 1024, 2048
OUT_TN = 1024
PERM_TOK = 256
RET_TQ = 512
MEM_TQ = 1024
FINAL_TM = 512


def _params(semantics):
    return pltpu.CompilerParams(dimension_semantics=semantics, vmem_limit_bytes=VMEM_LIMIT_BYTES)


def _dot(a, b):
    return jnp.dot(a, b, preferred_element_type=F32)


def _dot_nt(a, b):
    return lax.dot_general(a, b, (((1,), (1,)), ((), ())), preferred_element_type=F32)


def _silu(x):
    return x * (1.0 / (1.0 + jnp.exp(-x)))


def _residue_order_matrix(dil):
    per_class = PERM_TOK // dil
    dst = lax.broadcasted_iota(jnp.int32, (PERM_TOK, PERM_TOK), 0)
    src = lax.broadcasted_iota(jnp.int32, (PERM_TOK, PERM_TOK), 1)
    wanted = (dst & (per_class - 1)) * dil + (dst >> int(math.log2(per_class)))
    return jnp.where(src == wanted, 1.0, 0.0).astype(BF16)


def _norm_matmul_kernel(x_ref, g_ref, w_ref, o_ref, xn_ref, *, dils, tile_copy):
    j = pl.program_id(1)

    @pl.when(j == 0)
    def _normalize():
        def body(c, carry):
            r = pl.multiple_of(c * NORM_ROW_CHUNK, NORM_ROW_CHUNK)
            xc = x_ref[pl.ds(r, NORM_ROW_CHUNK), :]
            ms = jnp.mean(xc * xc, axis=-1, keepdims=True)
            xn_ref[0, pl.ds(r, NORM_ROW_CHUNK), :] = (xc * lax.rsqrt(ms + EPS) * g_ref[...]).astype(BF16)
            return carry

        lax.fori_loop(0, x_ref.shape[0] // NORM_ROW_CHUNK, body, 0, unroll=2)
        for v in range(1, len(dils)):
            order = _residue_order_matrix(dils[v])
            for b in range(x_ref.shape[0] // PERM_TOK):
                rows = slice(b * PERM_TOK, (b + 1) * PERM_TOK)
                xn_ref[v, rows, :] = _dot(order, xn_ref[0, rows, :]).astype(BF16)

    copy = 0
    for t, v in enumerate(tile_copy):
        if v:
            copy = copy + v * (j == t).astype(jnp.int32)
    o_ref[...] = _dot(xn_ref[copy], w_ref[...]).astype(o_ref.dtype)


def norm_matmul(x, g, g_idx, w, w_idx, *, name, dils=(1,), tile_copy=None):
    M, K = x.shape
    N = w.shape[-1]
    tm, tn = min(PROJ_TM, M), min(PROJ_TN, N)
    tile_copy = tile_copy or (0,) * (N // tn)
    assert dils[0] == 1 and len(tile_copy) == N // tn and tm % PERM_TOK == 0
    return pl.pallas_call(
        functools.partial(_norm_matmul_kernel, dils=dils, tile_copy=tile_copy),
        grid=(M // tm, N // tn),
        in_specs=[
            pl.BlockSpec((tm, K), lambda i, j: (i, 0)),
            pl.BlockSpec((None, 1, K), lambda i, j: (g_idx, 0, 0)),
            pl.BlockSpec((None, K, tn), lambda i, j: (w_idx, 0, j)),
        ],
        out_specs=pl.BlockSpec((tm, tn), lambda i, j: (i, j)),
        out_shape=jax.ShapeDtypeStruct((M, N), BF16),
        scratch_shapes=[pltpu.VMEM((len(dils), tm, K), BF16)],
        compiler_params=_params(("parallel", "arbitrary")),
        name=name,
    )(x, g, w)


def _out_proj_kernel(om_ref, omem_ref, w_ref, x_ref, o_ref):
    k_main = om_ref.shape[1]
    acc = _dot(om_ref[...], w_ref[:k_main, :]) + _dot(omem_ref[...], w_ref[k_main:, :])
    o_ref[...] = x_ref[...] + acc


def out_proj(o_main, o_mem, w, layer, x, *, name):
    M, k_main = o_main.shape
    k_mem = o_mem.shape[1]
    N = w.shape[-1]
    tm, tn = PROJ_TM, OUT_TN
    return pl.pallas_call(
        _out_proj_kernel,
        grid=(M // tm, N // tn),
        in_specs=[
            pl.BlockSpec((tm, k_main), lambda i, j: (i, 0)),
            pl.BlockSpec((tm, k_mem), lambda i, j: (i, 0)),
            pl.BlockSpec((None, k_main + k_mem, tn), lambda i, j: (layer, 0, j)),
            pl.BlockSpec((tm, tn), lambda i, j: (i, j)),
        ],
        out_specs=pl.BlockSpec((tm, tn), lambda i, j: (i, j)),
        out_shape=jax.ShapeDtypeStruct((M, N), F32),
        compiler_params=_params(("parallel", "parallel")),
        name=name,
    )(o_main, o_mem, w, x)


def _out_proj_norm_kernel(om_ref, omem_ref, w_ref, x_ref, g_ref, o_ref):
    k_main = om_ref.shape[1]
    o_ref[...] = x_ref[...] + (_dot(om_ref[...], w_ref[:k_main, :]) + _dot(omem_ref[...], w_ref[k_main:, :]))

    def body(c, carry):
        r = pl.multiple_of(c * NORM_ROW_CHUNK, NORM_ROW_CHUNK)
        xc = o_ref[pl.ds(r, NORM_ROW_CHUNK), :]
        ms = jnp.mean(xc * xc, axis=-1, keepdims=True)
        o_ref[pl.ds(r, NORM_ROW_CHUNK), :] = xc * lax.rsqrt(ms + EPS) * g_ref[...]
        return carry

    lax.fori_loop(0, o_ref.shape[0] // NORM_ROW_CHUNK, body, 0, unroll=2)


def out_proj_norm(o_main, o_mem, w, layer, x, g, *, name):
    M, k_main = o_main.shape
    k_mem = o_mem.shape[1]
    N = w.shape[-1]
    tm = FINAL_TM
    return pl.pallas_call(
        _out_proj_norm_kernel,
        grid=(M // tm,),
        in_specs=[
            pl.BlockSpec((tm, k_main), lambda i: (i, 0)),
            pl.BlockSpec((tm, k_mem), lambda i: (i, 0)),
            pl.BlockSpec((None, k_main + k_mem, N), lambda i: (layer, 0, 0)),
            pl.BlockSpec((tm, N), lambda i: (i, 0)),
            pl.BlockSpec((1, N), lambda i: (0, 0)),
        ],
        out_specs=pl.BlockSpec((tm, N), lambda i: (i, 0)),
        out_shape=jax.ShapeDtypeStruct((M, N), F32),
        compiler_params=_params(("parallel",)),
        name=name,
    )(o_main, o_mem, w, x, g)


def _retention_kernel(g_ref, q_ref, k_ref, v_ref, gate_ref, cos_ref, sin_ref, decay_ref, xi_ref, zeta_ref,
                      o_ref, state_ref):
    @pl.when(pl.program_id(1) == 0)
    def _reset():
        state_ref[...] = jnp.zeros_like(state_ref)

    def rotate(t, cos, sin):
        return t * cos + pltpu.roll(t, RET_DK // 2, axis=1) * sin

    for c in range(q_ref.shape[0] // RET_CHUNK):
        rows = pl.ds(c * RET_CHUNK, RET_CHUNK)
        cos = cos_ref[rows, :]
        sin = sin_ref[rows, :]
        for h in range(RET_HEADS):
            qk_cols = slice(h * RET_DK, (h + 1) * RET_DK)
            v_cols = slice(h * RET_DV, (h + 1) * RET_DV)
            q = rotate(q_ref[rows, qk_cols].astype(F32), cos, sin)
            k = rotate(k_ref[rows, qk_cols].astype(F32), cos, sin) * (RET_DK ** -0.5)
            v = v_ref[rows, v_cols]
            qb = q.astype(BF16)
            s = _dot_nt(qb, k.astype(BF16)) * decay_ref[h]
            state = state_ref[h]
            o = _dot(s.astype(BF16), v) + _dot(qb, state.astype(BF16)) * xi_ref[h]
            kz_t = (k * zeta_ref[h]).T.astype(BF16)
            state_ref[h] = g_ref[h] * state + _dot(kz_t, v)
            o = o * lax.rsqrt(jnp.mean(o * o, axis=-1, keepdims=True) + EPS)
            o_ref[rows, v_cols] = (o * _silu(gate_ref[rows, v_cols].astype(F32))).astype(o_ref.dtype)


def _rotary_tables(S):
    half = RET_DK // 2
    inv = 1.0 / (RET_THETA_BASE ** jnp.linspace(0.0, 1.0, half, dtype=F32))
    ang = jnp.arange(S, dtype=F32)[:, None] * inv[None, :]
    cos, sin = jnp.cos(ang), jnp.sin(ang)
    return jnp.concatenate([cos, cos], axis=-1), jnp.concatenate([-sin, sin], axis=-1)


def _decay_tables():
    H, C = RET_HEADS, RET_CHUNK
    log_g = jnp.log1p(-(2.0 ** (-5.0 - jnp.arange(H, dtype=F32))))
    pos = jnp.arange(C, dtype=F32)
    rel = pos[:, None] - pos[None, :]
    decay_in = jnp.where(rel >= 0, jnp.exp(jnp.maximum(rel, 0.0)[None] * log_g[:, None, None]), 0.0)
    xi = jnp.exp((pos[None, :] + 1.0) * log_g[:, None])
    zeta = jnp.exp((C - 1.0 - pos[None, :]) * log_g[:, None])
    g_chunk = jnp.exp(C * log_g)
    return g_chunk, decay_in, xi[:, :, None], zeta[:, :, None]


def retention_gated(proj, B, S):
    M = B * S
    tq = RET_TQ
    spb = S // tq
    g_chunk, decay_in, xi, zeta = _decay_tables()
    cos, sin = _rotary_tables(S)
    k_blk = 1
    v_blk = 2 * RET_QK_W // RET_V_W
    gate_blk = v_blk + 1
    row = lambda b, i: b * spb + i
    const3 = lambda b, i: (0, 0, 0)
    return pl.pallas_call(
        _retention_kernel,
        grid=(B, spb),
        in_specs=[
            pl.BlockSpec(memory_space=pltpu.SMEM),
            pl.BlockSpec((tq, RET_QK_W), lambda b, i: (row(b, i), 0)),
            pl.BlockSpec((tq, RET_QK_W), lambda b, i: (row(b, i), k_blk)),
            pl.BlockSpec((tq, RET_V_W), lambda b, i: (row(b, i), v_blk)),
            pl.BlockSpec((tq, RET_V_W), lambda b, i: (row(b, i), gate_blk)),
            pl.BlockSpec((tq, RET_DK), lambda b, i: (i, 0)),
            pl.BlockSpec((tq, RET_DK), lambda b, i: (i, 0)),
            pl.BlockSpec((RET_HEADS, RET_CHUNK, RET_CHUNK), const3),
            pl.BlockSpec((RET_HEADS, RET_CHUNK, 1), const3),
            pl.BlockSpec((RET_HEADS, RET_CHUNK, 1), const3),
        ],
        out_specs=pl.BlockSpec((tq, RET_V_W), lambda b, i: (row(b, i), 0)),
        out_shape=jax.ShapeDtypeStruct((M, RET_V_W), BF16),
        scratch_shapes=[pltpu.VMEM((RET_HEADS, RET_DK, RET_DV), F32)],
        compiler_params=_params(("parallel", "arbitrary")),
        name="retention",
    )(g_chunk, proj, proj, proj, proj, cos, sin, decay_in, xi, zeta)


def _mem_attn_kernel(q_ref, gate_ref, k_ref, v_ref, o_ref):
    for h in range(MEM_HEADS):
        cols = slice(h * MEM_DH, (h + 1) * MEM_DH)
        s = _dot_nt(q_ref[:, cols], k_ref[:, cols])
        p = jnp.exp2((s - jnp.max(s, axis=-1, keepdims=True)) * (MEM_DH ** -0.5 * math.log2(math.e)))
        l = jnp.sum(p, axis=-1, keepdims=True)
        o = _dot(p.astype(BF16), v_ref[:, cols]) * (1.0 / l)
        o_ref[:, cols] = (o * _silu(gate_ref[:, cols].astype(F32))).astype(o_ref.dtype)


def mem_attention_gated(proj, mem_kv, B, S, q_blk):
    M = B * S
    tq = MEM_TQ
    spb = S // tq
    row = lambda b, i: b * spb + i
    return pl.pallas_call(
        _mem_attn_kernel,
        grid=(B, spb),
        in_specs=[
            pl.BlockSpec((tq, MEM_W), lambda b, i: (row(b, i), q_blk)),
            pl.BlockSpec((tq, MEM_W), lambda b, i: (row(b, i), q_blk + 1)),
            pl.BlockSpec((MEM_LEN, MEM_W), lambda b, i: (b, 0)),
            pl.BlockSpec((MEM_LEN, MEM_W), lambda b, i: (b, 1)),
        ],
        out_specs=pl.BlockSpec((tq, MEM_W), lambda b, i: (row(b, i), 0)),
        out_shape=jax.ShapeDtypeStruct((M, MEM_W), BF16),
        compiler_params=_params(("parallel", "parallel")),
        name="mem_attention",
    )(proj, proj, mem_kv, mem_kv)


DIL_TOK = max(d for _, d in DIL_CONFIG) * DIL_BLOCK
DIL_KEYS = 2 * DIL_BLOCK


def _residue_rows(dil):
    per_class = DIL_TOK // dil
    return per_class, DIL_BLOCK + per_class


def _class_pieces(dil, r, pos0, count):
    per_perm = PERM_TOK // dil
    pieces = []
    while count:
        blk, off = divmod(pos0, per_perm)
        size = min(count, per_perm - off)
        pieces.append((blk * PERM_TOK + r * per_perm + off, size))
        pos0, count = pos0 + size, count - size
    return pieces


def _dilated_kernel(q0_ref, q1_ref, q2_ref, gate_ref, k0_ref, v0_ref, k1_ref, v1_ref, k2_ref, v2_ref, o_ref,
                    ke0_ref, ve0_ref, ke1_ref, ve1_ref, ke2_ref, ve2_ref, acc_ref, m_ref, l_ref):
    j = pl.program_id(2)
    q_refs = (q0_ref, q1_ref, q2_ref)
    kv_refs = ((k0_ref, v0_ref, ke0_ref, ve0_ref), (k1_ref, v1_ref, ke1_ref, ve1_ref), (k2_ref, v2_ref, ke2_ref, ve2_ref))

    for (k_ref, v_ref, ke_ref, ve_ref), (_, dil) in zip(kv_refs, DIL_CONFIG):
        per_class, seg = _residue_rows(dil)

        @pl.when(j == 0)
        def _zero_halo(ke_ref=ke_ref, ve_ref=ve_ref, dil=dil, seg=seg):
            for r in range(dil):
                ke_ref[r * seg:r * seg + DIL_BLOCK, :] = jnp.zeros((DIL_BLOCK, DIL_DH), BF16)
                ve_ref[r * seg:r * seg + DIL_BLOCK, :] = jnp.zeros((DIL_BLOCK, DIL_DH), BF16)

        @pl.when(j > 0)
        def _carry_halo(ke_ref=ke_ref, ve_ref=ve_ref, dil=dil, seg=seg, per_class=per_class):
            for r in range(dil):
                ke_ref[r * seg:r * seg + DIL_BLOCK, :] = ke_ref[r * seg + per_class:(r + 1) * seg, :]
                ve_ref[r * seg:r * seg + DIL_BLOCK, :] = ve_ref[r * seg + per_class:(r + 1) * seg, :]

        for r in range(dil):
            dst = r * seg + DIL_BLOCK
            for start, size in _class_pieces(dil, r, 0, per_class):
                ke_ref[dst:dst + size, :] = k_ref[start:start + size, :]
                ve_ref[dst:dst + size, :] = v_ref[start:start + size, :]
                dst += size

    key_lead = (lax.broadcasted_iota(jnp.int32, (DIL_BLOCK, DIL_BLOCK), 1)
                - lax.broadcasted_iota(jnp.int32, (DIL_BLOCK, DIL_BLOCK), 0))
    first_min_lead = jnp.where(j == 0, DIL_BLOCK, 0)
    exp2_scale = DIL_DH ** -0.5 * math.log2(math.e)

    def attend(q, ke_ref, ve_ref, key0, prev_min_lead, tok, first_group, last_group):
        kc = ke_ref[pl.ds(key0, DIL_KEYS), :]
        vc = ve_ref[pl.ds(key0, DIL_KEYS), :]
        s = _dot_nt(q, kc)
        s_prev = jnp.where(key_lead >= prev_min_lead, s[:, :DIL_BLOCK], -1e30)
        s_cur = jnp.where(key_lead <= 0, s[:, DIL_BLOCK:], -1e30)
        m_blk = jnp.max(jnp.maximum(s_prev, s_cur), axis=-1, keepdims=True)
        if first_group:
            m_new = jnp.broadcast_to(m_blk, (DIL_BLOCK, DIL_DH))
        else:
            m_old = m_ref[tok, :]
            m_new = jnp.maximum(m_old, m_blk)
            alpha = jnp.exp2((m_old - m_new) * exp2_scale)
        p_prev = jnp.exp2((s_prev - m_new) * exp2_scale)
        p_cur = jnp.exp2((s_cur - m_new) * exp2_scale)
        l = jnp.sum(p_prev + p_cur, axis=-1, keepdims=True)
        acc = _dot(jnp.concatenate([p_prev, p_cur], axis=1).astype(BF16), vc)
        if not first_group:
            l = alpha * l_ref[tok, :] + l
            acc = alpha * acc_ref[tok, :] + acc
        if last_group:
            o = acc * (1.0 / l) * _silu(gate_ref[tok, :].astype(F32))
            o_ref[tok, :] = o.astype(o_ref.dtype)
        else:
            l_ref[tok, :] = jnp.broadcast_to(l, (DIL_BLOCK, DIL_DH))
            acc_ref[tok, :] = acc
            m_ref[tok, :] = m_new

    groups = sorted(range(len(DIL_CONFIG)), key=lambda g: -DIL_CONFIG[g][1])
    for group in groups:
        dil = DIL_CONFIG[group][1]
        per_class, seg = _residue_rows(dil)
        _, _, ke_ref, ve_ref = kv_refs[group]
        for r in range(dil):
            for blk in range(per_class // DIL_BLOCK):
                pieces = _class_pieces(dil, r, blk * DIL_BLOCK, DIL_BLOCK)
                q = jnp.concatenate([q_refs[group][start:start + size, :] for start, size in pieces], axis=0)
                if dil == 1:
                    tok = pl.ds(blk * DIL_BLOCK, DIL_BLOCK)
                else:
                    tok = pl.ds(blk * DIL_BLOCK * dil + r, DIL_BLOCK, stride=dil)
                attend(q, ke_ref, ve_ref, r * seg + blk * DIL_BLOCK, first_min_lead if blk == 0 else 0, tok,
                       group == groups[0], group == groups[-1])


def dilated_gated(proj, kv, B, S):
    assert S % DIL_TOK == 0
    nblk = S // DIL_TOK
    heads_per_group = DIL_W // DIL_DH

    def head_cols(group_col):
        return pl.BlockSpec((DIL_TOK, DIL_DH), lambda b, h, j: (b * nblk + j, group_col * heads_per_group + h))

    ext = lambda dil: pltpu.VMEM((dil * _residue_rows(dil)[1], DIL_DH), BF16)
    tok_f32 = pltpu.VMEM((DIL_TOK, DIL_DH), F32)
    return pl.pallas_call(
        _dilated_kernel,
        grid=(B, DIL_HEADS, nblk),
        in_specs=[head_cols(c) for c in range(4)] + [head_cols(c) for c in range(6)],
        out_specs=pl.BlockSpec((DIL_TOK, DIL_DH), lambda b, h, j: (b * nblk + j, h)),
        out_shape=jax.ShapeDtypeStruct((B * S, DIL_W), BF16),
        scratch_shapes=[ext(d) for _, d in DIL_CONFIG for _ in range(2)] + [tok_f32, tok_f32, tok_f32],
        compiler_params=_params(("arbitrary", "arbitrary", "arbitrary")),
        name="dilated_attention",
    )(proj, proj, proj, proj, kv, kv, kv, kv, kv, kv)


def kernel(x, mem, norm_a, w_in_a, w_out_a, norm_b, w_in_b, w_out_b, w_mem_kv, mem_norm_g, kv_norm_g, w_kv, final_norm_g):
    B, S, D = x.shape
    xr = x.reshape(B * S, D)
    mem_r = mem.reshape(B * mem.shape[1], D)

    w_in_a, w_out_a, w_in_b, w_out_b, w_mem_kv = (t.astype(BF16) for t in (w_in_a, w_out_a, w_in_b, w_out_b, w_mem_kv))
    w_kv = w_kv.astype(BF16)[None]
    norm_a = norm_a[:, None, :]
    norm_b = norm_b[:, None, :]
    mem_g = mem_norm_g[None, None, :]
    kv_g = kv_norm_g[None, None, :]

    dils = tuple(d for _, d in DIL_CONFIG)
    tiles_per_group = DIL_W // PROJ_TN
    group_copies = tuple(range(len(DIL_CONFIG)))
    in_b_copies = tuple(c for c in group_copies + (0,) * ((IN_W_B - len(DIL_CONFIG) * DIL_W) // DIL_W)
                        for _ in range(tiles_per_group))
    kv_copies = tuple(c for c in group_copies for _ in range(2 * tiles_per_group))

    kv = None
    for layer in range(DEPTH):
        mem_kv = norm_matmul(mem_r, mem_g, 0, w_mem_kv, layer, name=f"mem_kv_proj{layer}")
        if layer < N_A:
            proj = norm_matmul(xr, norm_a, layer, w_in_a, layer, name=f"in_proj_a{layer}")
            o_main = retention_gated(proj, B, S)
            o_mem = mem_attention_gated(proj, mem_kv, B, S, (IN_W_A - 2 * MEM_W) // MEM_W)
            xr = out_proj(o_main, o_mem, w_out_a, layer, xr, name=f"out_proj_a{layer}")
            if layer == N_A - 1:
                kv = norm_matmul(xr, kv_g, 0, w_kv, 0, name="shared_kv_proj", dils=dils, tile_copy=kv_copies)
        else:
            jb = layer - N_A
            proj = norm_matmul(xr, norm_b, jb, w_in_b, jb, name=f"in_proj_b{jb}", dils=dils, tile_copy=in_b_copies)
            o_main = dilated_gated(proj, kv, B, S)
            o_mem = mem_attention_gated(proj, mem_kv, B, S, (IN_W_B - 2 * MEM_W) // MEM_W)
            if layer < DEPTH - 1:
                xr = out_proj(o_main, o_mem, w_out_b, jb, xr, name=f"out_proj_b{jb}")
            else:
                xr = out_proj_norm(o_main, o_mem, w_out_b, jb, xr, final_norm_g[None, :], name="out_proj_final_norm")
    return xr.reshape(B, S, D)
```

```python
import functools
import math

import jax
import jax.numpy as jnp
from jax import lax
from jax.experimental import pallas as pl
from jax.experimental.pallas import tpu as pltpu

F32 = jnp.float32
BF16 = jnp.bfloat16

D_MODEL = 2048
DEPTH = 4
N_A = DEPTH // 2
MEM_LEN = 256
RET_HEADS = 8
RET_DK = D_MODEL // 16
RET_DV = 2 * RET_DK
RET_CHUNK = 128
RET_THETA_BASE = 10000.0
DIL_CONFIG = ((128, 1), (512, 4), (2048, 16))
DIL_HEADS = 16
DIL_DH = D_MODEL // 16
DIL_BLOCK = 128
MEM_HEADS = 4
MEM_DH = D_MODEL // 8
EPS = 1e-6

RET_QK_W = RET_HEADS * RET_DK
RET_V_W = RET_HEADS * RET_DV
DIL_W = DIL_HEADS * DIL_DH
MEM_W = MEM_HEADS * MEM_DH
IN_W_A = 2 * RET_QK_W + 2 * RET_V_W + 2 * MEM_W
IN_W_B = len(DIL_CONFIG) * DIL_W + DIL_W + 2 * MEM_W
KV_W = 2 * len(DIL_CONFIG) * DIL_W

VMEM_LIMIT_BYTES = 62 * 2**20
NORM_ROW_CHUNK = 64
PROJ_TM, PROJ_TN = 1024, 2048
OUT_TN = 1024
PERM_TOK = 256
RET_TQ = 1024
MEM_TQ = 2048
FINAL_TM = 512


def _params(semantics):
    return pltpu.CompilerParams(dimension_semantics=semantics, vmem_limit_bytes=VMEM_LIMIT_BYTES)


def _dot(a, b):
    return jnp.dot(a, b, preferred_element_type=F32)


def _dot_nt(a, b):
    return lax.dot_general(a, b, (((1,), (1,)), ((), ())), preferred_element_type=F32)


def _silu(x):
    return x * (1.0 / (1.0 + jnp.exp(-x)))


def _residue_order_matrix(dil):
    per_class = PERM_TOK // dil
    dst = lax.broadcasted_iota(jnp.int32, (PERM_TOK, PERM_TOK), 0)
    src = lax.broadcasted_iota(jnp.int32, (PERM_TOK, PERM_TOK), 1)
    wanted = (dst & (per_class - 1)) * dil + (dst >> int(math.log2(per_class)))
    return jnp.where(src == wanted, 1.0, 0.0).astype(BF16)


def _norm_matmul_kernel(x_ref, g_ref, w_ref, o_ref, xn_ref, *, dils, tile_copy):
    j = pl.program_id(1)

    @pl.when(j == 0)
    def _normalize():
        def body(c, carry):
            r = pl.multiple_of(c * NORM_ROW_CHUNK, NORM_ROW_CHUNK)
            xc = x_ref[pl.ds(r, NORM_ROW_CHUNK), :]
            ms = jnp.mean(xc * xc, axis=-1, keepdims=True)
            xn_ref[0, pl.ds(r, NORM_ROW_CHUNK), :] = (xc * lax.rsqrt(ms + EPS) * g_ref[...]).astype(BF16)
            return carry

        lax.fori_loop(0, x_ref.shape[0] // NORM_ROW_CHUNK, body, 0, unroll=2)
        for v in range(1, len(dils)):
            order = _residue_order_matrix(dils[v])
            for b in range(x_ref.shape[0] // PERM_TOK):
                rows = slice(b * PERM_TOK, (b + 1) * PERM_TOK)
                xn_ref[v, rows, :] = _dot(order, xn_ref[0, rows, :]).astype(BF16)

    copy = 0
    for t, v in enumerate(tile_copy):
        if v:
            copy = copy + v * (j == t).astype(jnp.int32)
    o_ref[...] = _dot(xn_ref[copy], w_ref[...]).astype(o_ref.dtype)


def norm_matmul(x, g, g_idx, w, w_idx, *, name, dils=(1,), tile_copy=None):
    M, K = x.shape
    N = w.shape[-1]
    tm, tn = min(PROJ_TM, M), min(PROJ_TN, N)
    tile_copy = tile_copy or (0,) * (N // tn)
    assert dils[0] == 1 and len(tile_copy) == N // tn and tm % PERM_TOK == 0
    return pl.pallas_call(
        functools.partial(_norm_matmul_kernel, dils=dils, tile_copy=tile_copy),
        grid=(M // tm, N // tn),
        in_specs=[
            pl.BlockSpec((tm, K), lambda i, j: (i, 0)),
            pl.BlockSpec((None, 1, K), lambda i, j: (g_idx, 0, 0)),
            pl.BlockSpec((None, K, tn), lambda i, j: (w_idx, 0, j)),
        ],
        out_specs=pl.BlockSpec((tm, tn), lambda i, j: (i, j)),
        out_shape=jax.ShapeDtypeStruct((M, N), BF16),
        scratch_shapes=[pltpu.VMEM((len(dils), tm, K), BF16)],
        compiler_params=_params(("parallel", "arbitrary")),
        name=name,
    )(x, g, w)


def _out_proj_kernel(om_ref, omem_ref, w_ref, x_ref, o_ref):
    k_main = om_ref.shape[1]
    acc = _dot(om_ref[...], w_ref[:k_main, :]) + _dot(omem_ref[...], w_ref[k_main:, :])
    o_ref[...] = x_ref[...] + acc


def out_proj(o_main, o_mem, w, layer, x, *, name):
    M, k_main = o_main.shape
    k_mem = o_mem.shape[1]
    N = w.shape[-1]
    tm, tn = PROJ_TM, OUT_TN
    return pl.pallas_call(
        _out_proj_kernel,
        grid=(M // tm, N // tn),
        in_specs=[
            pl.BlockSpec((tm, k_main), lambda i, j: (i, 0)),
            pl.BlockSpec((tm, k_mem), lambda i, j: (i, 0)),
            pl.BlockSpec((None, k_main + k_mem, tn), lambda i, j: (layer, 0, j)),
            pl.BlockSpec((tm, tn), lambda i, j: (i, j)),
        ],
        out_specs=pl.BlockSpec((tm, tn), lambda i, j: (i, j)),
        out_shape=jax.ShapeDtypeStruct((M, N), F32),
        compiler_params=_params(("parallel", "parallel")),
        name=name,
    )(o_main, o_mem, w, x)


def _out_proj_norm_kernel(om_ref, omem_ref, w_ref, x_ref, g_ref, o_ref):
    k_main = om_ref.shape[1]
    o_ref[...] = x_ref[...] + (_dot(om_ref[...], w_ref[:k_main, :]) + _dot(omem_ref[...], w_ref[k_main:, :]))

    def body(c, carry):
        r = pl.multiple_of(c * NORM_ROW_CHUNK, NORM_ROW_CHUNK)
        xc = o_ref[pl.ds(r, NORM_ROW_CHUNK), :]
        ms = jnp.mean(xc * xc, axis=-1, keepdims=True)
        o_ref[pl.ds(r, NORM_ROW_CHUNK), :] = xc * lax.rsqrt(ms + EPS) * g_ref[...]
        return carry

    lax.fori_loop(0, o_ref.shape[0] // NORM_ROW_CHUNK, body, 0, unroll=2)


def out_proj_norm(o_main, o_mem, w, layer, x, g, *, name):
    M, k_main = o_main.shape
    k_mem = o_mem.shape[1]
    N = w.shape[-1]
    tm = FINAL_TM
    return pl.pallas_call(
        _out_proj_norm_kernel,
        grid=(M // tm,),
        in_specs=[
            pl.BlockSpec((tm, k_main), lambda i: (i, 0)),
            pl.BlockSpec((tm, k_mem), lambda i: (i, 0)),
            pl.BlockSpec((None, k_main + k_mem, N), lambda i: (layer, 0, 0)),
            pl.BlockSpec((tm, N), lambda i: (i, 0)),
            pl.BlockSpec((1, N), lambda i: (0, 0)),
        ],
        out_specs=pl.BlockSpec((tm, N), lambda i: (i, 0)),
        out_shape=jax.ShapeDtypeStruct((M, N), F32),
        compiler_params=_params(("parallel",)),
        name=name,
    )(o_main, o_mem, w, x, g)


def _retention_kernel(g_ref, q_ref, k_ref, v_ref, gate_ref, cos_ref, sin_ref, decay_ref, xi_ref, zeta_ref,
                      o_ref, state_ref):
    @pl.when(pl.program_id(1) == 0)
    def _reset():
        state_ref[...] = jnp.zeros_like(state_ref)

    def rotate(t, cos, sin):
        return t * cos + pltpu.roll(t, RET_DK // 2, axis=1) * sin

    for c in range(q_ref.shape[0] // RET_CHUNK):
        rows = pl.ds(c * RET_CHUNK, RET_CHUNK)
        cos = cos_ref[rows, :]
        sin = sin_ref[rows, :]
        for h in range(RET_HEADS):
            qk_cols = slice(h * RET_DK, (h + 1) * RET_DK)
            v_cols = slice(h * RET_DV, (h + 1) * RET_DV)
            q = rotate(q_ref[rows, qk_cols].astype(F32), cos, sin)
            k = rotate(k_ref[rows, qk_cols].astype(F32), cos, sin) * (RET_DK ** -0.5)
            v = v_ref[rows, v_cols]
            qb = q.astype(BF16)
            s = _dot_nt(qb, k.astype(BF16)) * decay_ref[h]
            state = state_ref[h]
            o = _dot(s.astype(BF16), v) + _dot(qb, state.astype(BF16)) * xi_ref[h]
            kz_t = (k * zeta_ref[h]).T.astype(BF16)
            state_ref[h] = g_ref[h] * state + _dot(kz_t, v)
            o = o * lax.rsqrt(jnp.mean(o * o, axis=-1, keepdims=True) + EPS)
            o_ref[rows, v_cols] = (o * _silu(gate_ref[rows, v_cols].astype(F32))).astype(o_ref.dtype)


def _rotary_tables(S):
    half = RET_DK // 2
    inv = 1.0 / (RET_THETA_BASE ** jnp.linspace(0.0, 1.0, half, dtype=F32))
    ang = jnp.arange(S, dtype=F32)[:, None] * inv[None, :]
    cos, sin = jnp.cos(ang), jnp.sin(ang)
    return jnp.concatenate([cos, cos], axis=-1), jnp.concatenate([-sin, sin], axis=-1)


def _decay_tables():
    H, C = RET_HEADS, RET_CHUNK
    log_g = jnp.log1p(-(2.0 ** (-5.0 - jnp.arange(H, dtype=F32))))
    pos = jnp.arange(C, dtype=F32)
    rel = pos[:, None] - pos[None, :]
    decay_in = jnp.where(rel >= 0, jnp.exp(jnp.maximum(rel, 0.0)[None] * log_g[:, None, None]), 0.0)
    xi = jnp.exp((pos[None, :] + 1.0) * log_g[:, None])
    zeta = jnp.exp((C - 1.0 - pos[None, :]) * log_g[:, None])
    g_chunk = jnp.exp(C * log_g)
    return g_chunk, decay_in, xi[:, :, None], zeta[:, :, None]


def retention_gated(proj, B, S):
    M = B * S
    tq = RET_TQ
    spb = S // tq
    g_chunk, decay_in, xi, zeta = _decay_tables()
    cos, sin = _rotary_tables(S)
    k_blk = 1
    v_blk = 2 * RET_QK_W // RET_V_W
    gate_blk = v_blk + 1
    row = lambda b, i: b * spb + i
    const3 = lambda b, i: (0, 0, 0)
    return pl.pallas_call(
        _retention_kernel,
        grid=(B, spb),
        in_specs=[
            pl.BlockSpec(memory_space=pltpu.SMEM),
            pl.BlockSpec((tq, RET_QK_W), lambda b, i: (row(b, i), 0)),
            pl.BlockSpec((tq, RET_QK_W), lambda b, i: (row(b, i), k_blk)),
            pl.BlockSpec((tq, RET_V_W), lambda b, i: (row(b, i), v_blk)),
            pl.BlockSpec((tq, RET_V_W), lambda b, i: (row(b, i), gate_blk)),
            pl.BlockSpec((tq, RET_DK), lambda b, i: (i, 0)),
            pl.BlockSpec((tq, RET_DK), lambda b, i: (i, 0)),
            pl.BlockSpec((RET_HEADS, RET_CHUNK, RET_CHUNK), const3),
            pl.BlockSpec((RET_HEADS, RET_CHUNK, 1), const3),
            pl.BlockSpec((RET_HEADS, RET_CHUNK, 1), const3),
        ],
        out_specs=pl.BlockSpec((tq, RET_V_W), lambda b, i: (row(b, i), 0)),
        out_shape=jax.ShapeDtypeStruct((M, RET_V_W), BF16),
        scratch_shapes=[pltpu.VMEM((RET_HEADS, RET_DK, RET_DV), F32)],
        compiler_params=_params(("parallel", "arbitrary")),
        name="retention",
    )(g_chunk, proj, proj, proj, proj, cos, sin, decay_in, xi, zeta)


def _mem_attn_kernel(q_ref, gate_ref, k_ref, v_ref, o_ref):
    for h in range(MEM_HEADS):
        cols = slice(h * MEM_DH, (h + 1) * MEM_DH)
        s = _dot_nt(q_ref[:, cols], k_ref[:, cols])
        p = jnp.exp2((s - jnp.max(s, axis=-1, keepdims=True)) * (MEM_DH ** -0.5 * math.log2(math.e)))
        l = jnp.sum(p, axis=-1, keepdims=True)
        o = _dot(p.astype(BF16), v_ref[:, cols]) * (1.0 / l)
        o_ref[:, cols] = (o * _silu(gate_ref[:, cols].astype(F32))).astype(o_ref.dtype)


def mem_attention_gated(proj, mem_kv, B, S, q_blk):
    M = B * S
    tq = MEM_TQ
    spb = S // tq
    row = lambda b, i: b * spb + i
    return pl.pallas_call(
        _mem_attn_kernel,
        grid=(B, spb),
        in_specs=[
            pl.BlockSpec((tq, MEM_W), lambda b, i: (row(b, i), q_blk)),
            pl.BlockSpec((tq, MEM_W), lambda b, i: (row(b, i), q_blk + 1)),
            pl.BlockSpec((MEM_LEN, MEM_W), lambda b, i: (b, 0)),
            pl.BlockSpec((MEM_LEN, MEM_W), lambda b, i: (b, 1)),
        ],
        out_specs=pl.BlockSpec((tq, MEM_W), lambda b, i: (row(b, i), 0)),
        out_shape=jax.ShapeDtypeStruct((M, MEM_W), BF16),
        compiler_params=_params(("parallel", "parallel")),
        name="mem_attention",
    )(proj, proj, mem_kv, mem_kv)


DIL_TOK = max(d for _, d in DIL_CONFIG) * DIL_BLOCK
DIL_KEYS = 2 * DIL_BLOCK


def _residue_rows(dil):
    per_class = DIL_TOK // dil
    return per_class, DIL_BLOCK + per_class


def _class_pieces(dil, r, pos0, count):
    per_perm = PERM_TOK // dil
    pieces = []
    while count:
        blk, off = divmod(pos0, per_perm)
        size = min(count, per_perm - off)
        pieces.append((blk * PERM_TOK + r * per_perm + off, size))
        pos0, count = pos0 + size, count - size
    return pieces


def _dilated_kernel(q0_ref, q1_ref, q2_ref, gate_ref, k0_ref, v0_ref, k1_ref, v1_ref, k2_ref, v2_ref, o_ref,
                    ke0_ref, ve0_ref, ke1_ref, ve1_ref, ke2_ref, ve2_ref, acc_ref, m_ref, l_ref):
    j = pl.program_id(2)
    q_refs = (q0_ref, q1_ref, q2_ref)
    kv_refs = ((k0_ref, v0_ref, ke0_ref, ve0_ref), (k1_ref, v1_ref, ke1_ref, ve1_ref), (k2_ref, v2_ref, ke2_ref, ve2_ref))

    for (k_ref, v_ref, ke_ref, ve_ref), (_, dil) in zip(kv_refs, DIL_CONFIG):
        per_class, seg = _residue_rows(dil)

        @pl.when(j == 0)
        def _zero_halo(ke_ref=ke_ref, ve_ref=ve_ref, dil=dil, seg=seg):
            for r in range(dil):
                ke_ref[r * seg:r * seg + DIL_BLOCK, :] = jnp.zeros((DIL_BLOCK, DIL_DH), BF16)
                ve_ref[r * seg:r * seg + DIL_BLOCK, :] = jnp.zeros((DIL_BLOCK, DIL_DH), BF16)

        @pl.when(j > 0)
        def _carry_halo(ke_ref=ke_ref, ve_ref=ve_ref, dil=dil, seg=seg, per_class=per_class):
            for r in range(dil):
                ke_ref[r * seg:r * seg + DIL_BLOCK, :] = ke_ref[r * seg + per_class:(r + 1) * seg, :]
                ve_ref[r * seg:r * seg + DIL_BLOCK, :] = ve_ref[r * seg + per_class:(r + 1) * seg, :]

        for r in range(dil):
            dst = r * seg + DIL_BLOCK
            for start, size in _class_pieces(dil, r, 0, per_class):
                ke_ref[dst:dst + size, :] = k_ref[start:start + size, :]
                ve_ref[dst:dst + size, :] = v_ref[start:start + size, :]
                dst += size

    key_lead = (lax.broadcasted_iota(jnp.int32, (DIL_BLOCK, DIL_BLOCK), 1)
                - lax.broadcasted_iota(jnp.int32, (DIL_BLOCK, DIL_BLOCK), 0))
    first_min_lead = jnp.where(j == 0, DIL_BLOCK, 0)
    exp2_scale = DIL_DH ** -0.5 * math.log2(math.e)

    def attend(q, ke_ref, ve_ref, key0, prev_min_lead, tok, first_group, last_group):
        kc = ke_ref[pl.ds(key0, DIL_KEYS), :]
        vc = ve_ref[pl.ds(key0, DIL_KEYS), :]
        s = _dot_nt(q, kc)
        s_prev = jnp.where(key_lead >= prev_min_lead, s[:, :DIL_BLOCK], -1e30)
        s_cur = jnp.where(key_lead <= 0, s[:, DIL_BLOCK:], -1e30)
        m_blk = jnp.max(jnp.maximum(s_prev, s_cur), axis=-1, keepdims=True)
        if first_group:
            m_new = jnp.broadcast_to(m_blk, (DIL_BLOCK, DIL_DH))
        else:
            m_old = m_ref[tok, :]
            m_new = jnp.maximum(m_old, m_blk)
            alpha = jnp.exp2((m_old - m_new) * exp2_scale)
        p_prev = jnp.exp2((s_prev - m_new) * exp2_scale)
        p_cur = jnp.exp2((s_cur - m_new) * exp2_scale)
        l = jnp.sum(p_prev + p_cur, axis=-1, keepdims=True)
        acc = _dot(jnp.concatenate([p_prev, p_cur], axis=1).astype(BF16), vc)
        if not first_group:
            l = alpha * l_ref[tok, :] + l
            acc = alpha * acc_ref[tok, :] + acc
        if last_group:
            o = acc * (1.0 / l) * _silu(gate_ref[tok, :].astype(F32))
            o_ref[tok, :] = o.astype(o_ref.dtype)
        else:
            l_ref[tok, :] = jnp.broadcast_to(l, (DIL_BLOCK, DIL_DH))
            acc_ref[tok, :] = acc
            m_ref[tok, :] = m_new

    groups = sorted(range(len(DIL_CONFIG)), key=lambda g: -DIL_CONFIG[g][1])
    for group in groups:
        dil = DIL_CONFIG[group][1]
        per_class, seg = _residue_rows(dil)
        _, _, ke_ref, ve_ref = kv_refs[group]
        for r in range(dil):
            for blk in range(per_class // DIL_BLOCK):
                pieces = _class_pieces(dil, r, blk * DIL_BLOCK, DIL_BLOCK)
                q = jnp.concatenate([q_refs[group][start:start + size, :] for start, size in pieces], axis=0)
                if dil == 1:
                    tok = pl.ds(blk * DIL_BLOCK, DIL_BLOCK)
                else:
                    tok = pl.ds(blk * DIL_BLOCK * dil + r, DIL_BLOCK, stride=dil)
                attend(q, ke_ref, ve_ref, r * seg + blk * DIL_BLOCK, first_min_lead if blk == 0 else 0, tok,
                       group == groups[0], group == groups[-1])


def dilated_gated(proj, kv, B, S):
    assert S % DIL_TOK == 0
    nblk = S // DIL_TOK
    heads_per_group = DIL_W // DIL_DH

    def head_cols(group_col):
        return pl.BlockSpec((DIL_TOK, DIL_DH), lambda b, h, j: (b * nblk + j, group_col * heads_per_group + h))

    ext = lambda dil: pltpu.VMEM((dil * _residue_rows(dil)[1], DIL_DH), BF16)
    tok_f32 = pltpu.VMEM((DIL_TOK, DIL_DH), F32)
    return pl.pallas_call(
        _dilated_kernel,
        grid=(B, DIL_HEADS, nblk),
        in_specs=[head_cols(c) for c in range(4)] + [head_cols(c) for c in range(6)],
        out_specs=pl.BlockSpec((DIL_TOK, DIL_DH), lambda b, h, j: (b * nblk + j, h)),
        out_shape=jax.ShapeDtypeStruct((B * S, DIL_W), BF16),
        scratch_shapes=[ext(d) for _, d in DIL_CONFIG for _ in range(2)] + [tok_f32, tok_f32, tok_f32],
        compiler_params=_params(("arbitrary", "arbitrary", "arbitrary")),
        name="dilated_attention",
    )(proj, proj, proj, proj, kv, kv, kv, kv, kv, kv)


def kernel(x, mem, norm_a, w_in_a, w_out_a, norm_b, w_in_b, w_out_b, w_mem_kv, mem_norm_g, kv_norm_g, w_kv, final_norm_g):
    B, S, D = x.shape
    xr = x.reshape(B * S, D)
    mem_r = mem.reshape(B * mem.shape[1], D)

    w_in_a, w_out_a, w_in_b, w_out_b, w_mem_kv = (t.astype(BF16) for t in (w_in_a, w_out_a, w_in_b, w_out_b, w_mem_kv))
    w_kv = w_kv.astype(BF16)[None]
    norm_a = norm_a[:, None, :]
    norm_b = norm_b[:, None, :]
    mem_g = mem_norm_g[None, None, :]
    kv_g = kv_norm_g[None, None, :]

    dils = tuple(d for _, d in DIL_CONFIG)
    tiles_per_group = DIL_W // PROJ_TN
    group_copies = tuple(range(len(DIL_CONFIG)))
    in_b_copies = tuple(c for c in group_copies + (0,) * ((IN_W_B - len(DIL_CONFIG) * DIL_W) // DIL_W)
                        for _ in range(tiles_per_group))
    kv_copies = tuple(c for c in group_copies for _ in range(2 * tiles_per_group))

    kv = None
    for layer in range(DEPTH):
        mem_kv = norm_matmul(mem_r, mem_g, 0, w_mem_kv, layer, name=f"mem_kv_proj{layer}")
        if layer < N_A:
            proj = norm_matmul(xr, norm_a, layer, w_in_a, layer, name=f"in_proj_a{layer}")
            o_main = retention_gated(proj, B, S)
            o_mem = mem_attention_gated(proj, mem_kv, B, S, (IN_W_A - 2 * MEM_W) // MEM_W)
            xr = out_proj(o_main, o_mem, w_out_a, layer, xr, name=f"out_proj_a{layer}")
            if layer == N_A - 1:
                kv = norm_matmul(xr, kv_g, 0, w_kv, 0, name="shared_kv_proj", dils=dils, tile_copy=kv_copies)
        else:
            jb = layer - N_A
            proj = norm_matmul(xr, norm_b, jb, w_in_b, jb, name=f"in_proj_b{jb}", dils=dils, tile_copy=in_b_copies)
            o_main = dilated_gated(proj, kv, B, S)
            o_mem = mem_attention_gated(proj, mem_kv, B, S, (IN_W_B - 2 * MEM_W) // MEM_W)
            if layer < DEPTH - 1:
                xr = out_proj(o_main, o_mem, w_out_b, jb, xr, name=f"out_proj_b{jb}")
            else:
                xr = out_proj_norm(o_main, o_mem, w_out_b, jb, xr, final_norm_g[None, :], name="out_proj_final_norm")
    return xr.reshape(B, S, D)
```

```python
import functools
import math

import jax
import jax.numpy as jnp
from jax import lax
from jax.experimental import pallas as pl
from jax.experimental.pallas import tpu as pltpu

F32 = jnp.float32
BF16 = jnp.bfloat16

D_MODEL = 2048
DEPTH = 4
N_A = DEPTH // 2
MEM_LEN = 256
RET_HEADS = 8
RET_DK = D_MODEL // 16
RET_DV = 2 * RET_DK
RET_CHUNK = 128
RET_THETA_BASE = 10000.0
DIL_CONFIG = ((128, 1), (512, 4), (2048, 16))
DIL_HEADS = 16
DIL_DH = D_MODEL // 16
DIL_BLOCK = 128
MEM_HEADS = 4
MEM_DH = D_MODEL // 8
EPS = 1e-6

RET_QK_W = RET_HEADS * RET_DK
RET_V_W = RET_HEADS * RET_DV
DIL_W = DIL_HEADS * DIL_DH
MEM_W = MEM_HEADS * MEM_DH
IN_W_A = 2 * RET_QK_W + 2 * RET_V_W + 2 * MEM_W
IN_W_B = len(DIL_CONFIG) * DIL_W + DIL_W + 2 * MEM_W
KV_W = 2 * len(DIL_CONFIG) * DIL_W

VMEM_LIMIT_BYTES = 62 * 2**20
NORM_ROW_CHUNK = 64
PROJ_TM, PROJ_TN = 1024, 2048
OUT_TN = 1024
PERM_TOK = 256
RET_TQ = 1024
MEM_TQ = 2048
FINAL_TM = 512


def _params(semantics):
    return pltpu.CompilerParams(dimension_semantics=semantics, vmem_limit_bytes=VMEM_LIMIT_BYTES)


def _dot(a, b):
    return jnp.dot(a, b, preferred_element_type=F32)


def _dot_nt(a, b):
    return lax.dot_general(a, b, (((1,), (1,)), ((), ())), preferred_element_type=F32)


def _silu(x):
    return x * (1.0 / (1.0 + jnp.exp(-x)))


def _residue_order_matrix(dil):
    per_class = PERM_TOK // dil
    dst = lax.broadcasted_iota(jnp.int32, (PERM_TOK, PERM_TOK), 0)
    src = lax.broadcasted_iota(jnp.int32, (PERM_TOK, PERM_TOK), 1)
    wanted = (dst & (per_class - 1)) * dil + (dst >> int(math.log2(per_class)))
    return jnp.where(src == wanted, 1.0, 0.0).astype(BF16)


def _norm_matmul_kernel(x_ref, g_ref, w_ref, o_ref, xn_ref, *, dils, tile_copy):
    j = pl.program_id(1)

    @pl.when(j == 0)
    def _normalize():
        def body(c, carry):
            r = pl.multiple_of(c * NORM_ROW_CHUNK, NORM_ROW_CHUNK)
            xc = x_ref[pl.ds(r, NORM_ROW_CHUNK), :]
            ms = jnp.mean(xc * xc, axis=-1, keepdims=True)
            xn_ref[0, pl.ds(r, NORM_ROW_CHUNK), :] = (xc * lax.rsqrt(ms + EPS) * g_ref[...]).astype(BF16)
            return carry

        lax.fori_loop(0, x_ref.shape[0] // NORM_ROW_CHUNK, body, 0, unroll=2)
        for v in range(1, len(dils)):
            order = _residue_order_matrix(dils[v])
            for b in range(x_ref.shape[0] // PERM_TOK):
                rows = slice(b * PERM_TOK, (b + 1) * PERM_TOK)
                xn_ref[v, rows, :] = _dot(order, xn_ref[0, rows, :]).astype(BF16)

    copy = 0
    for t, v in enumerate(tile_copy):
        if v:
            copy = copy + v * (j == t).astype(jnp.int32)
    o_ref[...] = _dot(xn_ref[copy], w_ref[...]).astype(o_ref.dtype)


def norm_matmul(x, g, g_idx, w, w_idx, *, name, dils=(1,), tile_copy=None):
    M, K = x.shape
    N = w.shape[-1]
    tm, tn = min(PROJ_TM, M), min(PROJ_TN, N)
    tile_copy = tile_copy or (0,) * (N // tn)
    assert dils[0] == 1 and len(tile_copy) == N // tn and tm % PERM_TOK == 0
    return pl.pallas_call(
        functools.partial(_norm_matmul_kernel, dils=dils, tile_copy=tile_copy),
        grid=(M // tm, N // tn),
        in_specs=[
            pl.BlockSpec((tm, K), lambda i, j: (i, 0)),
            pl.BlockSpec((None, 1, K), lambda i, j: (g_idx, 0, 0)),
            pl.BlockSpec((None, K, tn), lambda i, j: (w_idx, 0, j)),
        ],
        out_specs=pl.BlockSpec((tm, tn), lambda i, j: (i, j)),
        out_shape=jax.ShapeDtypeStruct((M, N), BF16),
        scratch_shapes=[pltpu.VMEM((len(dils), tm, K), BF16)],
        compiler_params=_params(("parallel", "arbitrary")),
        name=name,
    )(x, g, w)


def _out_proj_kernel(om_ref, omem_ref, w_ref, x_ref, o_ref):
    k_main = om_ref.shape[1]
    acc = _dot(om_ref[...], w_ref[:k_main, :]) + _dot(omem_ref[...], w_ref[k_main:, :])
    o_ref[...] = x_ref[...] + acc


def out_proj(o_main, o_mem, w, layer, x, *, name):
    M, k_main = o_main.shape
    k_mem = o_mem.shape[1]
    N = w.shape[-1]
    tm, tn = PROJ_TM, OUT_TN
    return pl.pallas_call(
        _out_proj_kernel,
        grid=(M // tm, N // tn),
        in_specs=[
            pl.BlockSpec((tm, k_main), lambda i, j: (i, 0)),
            pl.BlockSpec((tm, k_mem), lambda i, j: (i, 0)),
            pl.BlockSpec((None, k_main + k_mem, tn), lambda i, j: (layer, 0, j)),
            pl.BlockSpec((tm, tn), lambda i, j: (i, j)),
        ],
        out_specs=pl.BlockSpec((tm, tn), lambda i, j: (i, j)),
        out_shape=jax.ShapeDtypeStruct((M, N), F32),
        compiler_params=_params(("parallel", "parallel")),
        name=name,
    )(o_main, o_mem, w, x)


def _out_proj_norm_kernel(om_ref, omem_ref, w_ref, x_ref, g_ref, o_ref):
    k_main = om_ref.shape[1]
    o_ref[...] = x_ref[...] + (_dot(om_ref[...], w_ref[:k_main, :]) + _dot(omem_ref[...], w_ref[k_main:, :]))

    def body(c, carry):
        r = pl.multiple_of(c * NORM_ROW_CHUNK, NORM_ROW_CHUNK)
        xc = o_ref[pl.ds(r, NORM_ROW_CHUNK), :]
        ms = jnp.mean(xc * xc, axis=-1, keepdims=True)
        o_ref[pl.ds(r, NORM_ROW_CHUNK), :] = xc * lax.rsqrt(ms + EPS) * g_ref[...]
        return carry

    lax.fori_loop(0, o_ref.shape[0] // NORM_ROW_CHUNK, body, 0, unroll=2)


def out_proj_norm(o_main, o_mem, w, layer, x, g, *, name):
    M, k_main = o_main.shape
    k_mem = o_mem.shape[1]
    N = w.shape[-1]
    tm = FINAL_TM
    return pl.pallas_call(
        _out_proj_norm_kernel,
        grid=(M // tm,),
        in_specs=[
            pl.BlockSpec((tm, k_main), lambda i: (i, 0)),
            pl.BlockSpec((tm, k_mem), lambda i: (i, 0)),
            pl.BlockSpec((None, k_main + k_mem, N), lambda i: (layer, 0, 0)),
            pl.BlockSpec((tm, N), lambda i: (i, 0)),
            pl.BlockSpec((1, N), lambda i: (0, 0)),
        ],
        out_specs=pl.BlockSpec((tm, N), lambda i: (i, 0)),
        out_shape=jax.ShapeDtypeStruct((M, N), F32),
        compiler_params=_params(("parallel",)),
        name=name,
    )(o_main, o_mem, w, x, g)


def _retention_kernel(g_ref, q_ref, k_ref, v_ref, gate_ref, cos_ref, sin_ref, decay_ref, xi_ref, zeta_ref,
                      o_ref, state_ref):
    @pl.when(pl.program_id(1) == 0)
    def _reset():
        state_ref[...] = jnp.zeros_like(state_ref)

    def rotate(t, cos, sin):
        return t * cos + pltpu.roll(t, RET_DK // 2, axis=1) * sin

    for c in range(q_ref.shape[0] // RET_CHUNK):
        rows = pl.ds(c * RET_CHUNK, RET_CHUNK)
        cos = cos_ref[rows, :]
        sin = sin_ref[rows, :]
        for h in range(RET_HEADS):
            qk_cols = slice(h * RET_DK, (h + 1) * RET_DK)
            v_cols = slice(h * RET_DV, (h + 1) * RET_DV)
            q = rotate(q_ref[rows, qk_cols].astype(F32), cos, sin)
            k = rotate(k_ref[rows, qk_cols].astype(F32), cos, sin) * (RET_DK ** -0.5)
            v = v_ref[rows, v_cols]
            qb = q.astype(BF16)
            s = _dot_nt(qb, k.astype(BF16)) * decay_ref[h]
            state = state_ref[h]
            o = _dot(s.astype(BF16), v) + _dot(qb, state.astype(BF16)) * xi_ref[h]
            kz_t = (k * zeta_ref[h]).T.astype(BF16)
            state_ref[h] = g_ref[h] * state + _dot(kz_t, v)
            o = o * lax.rsqrt(jnp.mean(o * o, axis=-1, keepdims=True) + EPS)
            o_ref[rows, v_cols] = (o * _silu(gate_ref[rows, v_cols].astype(F32))).astype(o_ref.dtype)


def _rotary_tables(S):
    half = RET_DK // 2
    inv = 1.0 / (RET_THETA_BASE ** jnp.linspace(0.0, 1.0, half, dtype=F32))
    ang = jnp.arange(S, dtype=F32)[:, None] * inv[None, :]
    cos, sin = jnp.cos(ang), jnp.sin(ang)
    return jnp.concatenate([cos, cos], axis=-1), jnp.concatenate([-sin, sin], axis=-1)


def _decay_tables():
    H, C = RET_HEADS, RET_CHUNK
    log_g = jnp.log1p(-(2.0 ** (-5.0 - jnp.arange(H, dtype=F32))))
    pos = jnp.arange(C, dtype=F32)
    rel = pos[:, None] - pos[None, :]
    decay_in = jnp.where(rel >= 0, jnp.exp(jnp.maximum(rel, 0.0)[None] * log_g[:, None, None]), 0.0)
    xi = jnp.exp((pos[None, :] + 1.0) * log_g[:, None])
    zeta = jnp.exp((C - 1.0 - pos[None, :]) * log_g[:, None])
    g_chunk = jnp.exp(C * log_g)
    return g_chunk, decay_in, xi[:, :, None], zeta[:, :, None]


def retention_gated(proj, B, S):
    M = B * S
    tq = RET_TQ
    spb = S // tq
    g_chunk, decay_in, xi, zeta = _decay_tables()
    cos, sin = _rotary_tables(S)
    k_blk = 1
    v_blk = 2 * RET_QK_W // RET_V_W
    gate_blk = v_blk + 1
    row = lambda b, i: b * spb + i
    const3 = lambda b, i: (0, 0, 0)
    return pl.pallas_call(
        _retention_kernel,
        grid=(B, spb),
        in_specs=[
            pl.BlockSpec(memory_space=pltpu.SMEM),
            pl.BlockSpec((tq, RET_QK_W), lambda b, i: (row(b, i), 0)),
            pl.BlockSpec((tq, RET_QK_W), lambda b, i: (row(b, i), k_blk)),
            pl.BlockSpec((tq, RET_V_W), lambda b, i: (row(b, i), v_blk)),
            pl.BlockSpec((tq, RET_V_W), lambda b, i: (row(b, i), gate_blk)),
            pl.BlockSpec((tq, RET_DK), lambda b, i: (i, 0)),
            pl.BlockSpec((tq, RET_DK), lambda b, i: (i, 0)),
            pl.BlockSpec((RET_HEADS, RET_CHUNK, RET_CHUNK), const3),
            pl.BlockSpec((RET_HEADS, RET_CHUNK, 1), const3),
            pl.BlockSpec((RET_HEADS, RET_CHUNK, 1), const3),
        ],
        out_specs=pl.BlockSpec((tq, RET_V_W), lambda b, i: (row(b, i), 0)),
        out_shape=jax.ShapeDtypeStruct((M, RET_V_W), BF16),
        scratch_shapes=[pltpu.VMEM((RET_HEADS, RET_DK, RET_DV), F32)],
        compiler_params=_params(("parallel", "arbitrary")),
        name="retention",
    )(g_chunk, proj, proj, proj, proj, cos, sin, decay_in, xi, zeta)


def _mem_attn_kernel(q_ref, gate_ref, k_ref, v_ref, o_ref):
    for h in range(MEM_HEADS):
        cols = slice(h * MEM_DH, (h + 1) * MEM_DH)
        s = _dot_nt(q_ref[:, cols], k_ref[:, cols])
        p = jnp.exp2((s - jnp.max(s, axis=-1, keepdims=True)) * (MEM_DH ** -0.5 * math.log2(math.e)))
        l = jnp.sum(p, axis=-1, keepdims=True)
        o = _dot(p.astype(BF16), v_ref[:, cols]) * (1.0 / l)
        o_ref[:, cols] = (o * _silu(gate_ref[:, cols].astype(F32))).astype(o_ref.dtype)


def mem_attention_gated(proj, mem_kv, B, S, q_blk):
    M = B * S
    tq = MEM_TQ
    spb = S // tq
    row = lambda b, i: b * spb + i
    return pl.pallas_call(
        _mem_attn_kernel,
        grid=(B, spb),
        in_specs=[
            pl.BlockSpec((tq, MEM_W), lambda b, i: (row(b, i), q_blk)),
            pl.BlockSpec((tq, MEM_W), lambda b, i: (row(b, i), q_blk + 1)),
            pl.BlockSpec((MEM_LEN, MEM_W), lambda b, i: (b, 0)),
            pl.BlockSpec((MEM_LEN, MEM_W), lambda b, i: (b, 1)),
        ],
        out_specs=pl.BlockSpec((tq, MEM_W), lambda b, i: (row(b, i), 0)),
        out_shape=jax.ShapeDtypeStruct((M, MEM_W), BF16),
        compiler_params=_params(("parallel", "parallel")),
        name="mem_attention",
    )(proj, proj, mem_kv, mem_kv)


DIL_TOK = max(d for _, d in DIL_CONFIG) * DIL_BLOCK
DIL_KEYS = 2 * DIL_BLOCK
DIL_HEADS_PER_STEP = 2


def _residue_rows(dil):
    per_class = DIL_TOK // dil
    return per_class, DIL_BLOCK + per_class


def _class_pieces(dil, r, pos0, count):
    per_perm = PERM_TOK // dil
    pieces = []
    while count:
        blk, off = divmod(pos0, per_perm)
        size = min(count, per_perm - off)
        pieces.append((blk * PERM_TOK + r * per_perm + off, size))
        pos0, count = pos0 + size, count - size
    return pieces


def _dilated_kernel(*refs):
    n_in = 4 + 2 * len(DIL_CONFIG)
    ins, o_ref, ext, stats = refs[:n_in], refs[n_in], refs[n_in + 1:-3], refs[-3:]
    for head in range(DIL_HEADS_PER_STEP):
        cols = pl.ds(head * DIL_DH, DIL_DH)
        _dilated_head(*(r.at[:, cols] for r in ins), o_ref.at[:, cols], *(e.at[head] for e in ext), *stats)


def _dilated_head(q0_ref, q1_ref, q2_ref, gate_ref, k0_ref, v0_ref, k1_ref, v1_ref, k2_ref, v2_ref, o_ref,
                  ke0_ref, ve0_ref, ke1_ref, ve1_ref, ke2_ref, ve2_ref, acc_ref, m_ref, l_ref):
    j = pl.program_id(2)
    q_refs = (q0_ref, q1_ref, q2_ref)
    kv_refs = ((k0_ref, v0_ref, ke0_ref, ve0_ref), (k1_ref, v1_ref, ke1_ref, ve1_ref), (k2_ref, v2_ref, ke2_ref, ve2_ref))

    for (k_ref, v_ref, ke_ref, ve_ref), (_, dil) in zip(kv_refs, DIL_CONFIG):
        per_class, seg = _residue_rows(dil)

        @pl.when(j == 0)
        def _zero_halo(ke_ref=ke_ref, ve_ref=ve_ref, dil=dil, seg=seg):
            for r in range(dil):
                ke_ref[r * seg:r * seg + DIL_BLOCK, :] = jnp.zeros((DIL_BLOCK, DIL_DH), BF16)
                ve_ref[r * seg:r * seg + DIL_BLOCK, :] = jnp.zeros((DIL_BLOCK, DIL_DH), BF16)

        @pl.when(j > 0)
        def _carry_halo(ke_ref=ke_ref, ve_ref=ve_ref, dil=dil, seg=seg, per_class=per_class):
            for r in range(dil):
                ke_ref[r * seg:r * seg + DIL_BLOCK, :] = ke_ref[r * seg + per_class:(r + 1) * seg, :]
                ve_ref[r * seg:r * seg + DIL_BLOCK, :] = ve_ref[r * seg + per_class:(r + 1) * seg, :]

        for r in range(dil):
            dst = r * seg + DIL_BLOCK
            for start, size in _class_pieces(dil, r, 0, per_class):
                ke_ref[dst:dst + size, :] = k_ref[start:start + size, :]
                ve_ref[dst:dst + size, :] = v_ref[start:start + size, :]
                dst += size

    key_lead = (lax.broadcasted_iota(jnp.int32, (DIL_BLOCK, DIL_BLOCK), 1)
                - lax.broadcasted_iota(jnp.int32, (DIL_BLOCK, DIL_BLOCK), 0))
    first_min_lead = jnp.where(j == 0, DIL_BLOCK, 0)
    exp2_scale = DIL_DH ** -0.5 * math.log2(math.e)

    def attend(q, ke_ref, ve_ref, key0, prev_min_lead, tok, first_group, last_group):
        kc = ke_ref[pl.ds(key0, DIL_KEYS), :]
        vc = ve_ref[pl.ds(key0, DIL_KEYS), :]
        s = _dot_nt(q, kc)
        s_prev = jnp.where(key_lead >= prev_min_lead, s[:, :DIL_BLOCK], -1e30)
        s_cur = jnp.where(key_lead <= 0, s[:, DIL_BLOCK:], -1e30)
        m_blk = jnp.max(jnp.maximum(s_prev, s_cur), axis=-1, keepdims=True)
        if first_group:
            m_new = jnp.broadcast_to(m_blk, (DIL_BLOCK, DIL_DH))
        else:
            m_old = m_ref[tok, :]
            m_new = jnp.maximum(m_old, m_blk)
            alpha = jnp.exp2((m_old - m_new) * exp2_scale)
        p_prev = jnp.exp2((s_prev - m_new) * exp2_scale)
        p_cur = jnp.exp2((s_cur - m_new) * exp2_scale)
        l = jnp.sum(p_prev + p_cur, axis=-1, keepdims=True)
        acc = _dot(jnp.concatenate([p_prev, p_cur], axis=1).astype(BF16), vc)
        if not first_group:
            l = alpha * l_ref[tok, :] + l
            acc = alpha * acc_ref[tok, :] + acc
        if last_group:
            o = acc * (1.0 / l) * _silu(gate_ref[tok, :].astype(F32))
            o_ref[tok, :] = o.astype(o_ref.dtype)
        else:
            l_ref[tok, :] = jnp.broadcast_to(l, (DIL_BLOCK, DIL_DH))
            acc_ref[tok, :] = acc
            m_ref[tok, :] = m_new

    groups = sorted(range(len(DIL_CONFIG)), key=lambda g: -DIL_CONFIG[g][1])
    for group in groups:
        dil = DIL_CONFIG[group][1]
        per_class, seg = _residue_rows(dil)
        _, _, ke_ref, ve_ref = kv_refs[group]
        for r in range(dil):
            for blk in range(per_class // DIL_BLOCK):
                pieces = _class_pieces(dil, r, blk * DIL_BLOCK, DIL_BLOCK)
                q = jnp.concatenate([q_refs[group][start:start + size, :] for start, size in pieces], axis=0)
                if dil == 1:
                    tok = pl.ds(blk * DIL_BLOCK, DIL_BLOCK)
                else:
                    tok = pl.ds(blk * DIL_BLOCK * dil + r, DIL_BLOCK, stride=dil)
                attend(q, ke_ref, ve_ref, r * seg + blk * DIL_BLOCK, first_min_lead if blk == 0 else 0, tok,
                       group == groups[0], group == groups[-1])


def dilated_gated(proj, kv, B, S):
    assert S % DIL_TOK == 0
    nblk = S // DIL_TOK
    width = DIL_HEADS_PER_STEP * DIL_DH
    steps_per_group = DIL_W // width

    def head_cols(group_col):
        return pl.BlockSpec((DIL_TOK, width), lambda b, h, j: (b * nblk + j, group_col * steps_per_group + h))

    ext = lambda dil: pltpu.VMEM((DIL_HEADS_PER_STEP, dil * _residue_rows(dil)[1], DIL_DH), BF16)
    tok_f32 = pltpu.VMEM((DIL_TOK, DIL_DH), F32)
    return pl.pallas_call(
        _dilated_kernel,
        grid=(B, DIL_HEADS // DIL_HEADS_PER_STEP, nblk),
        in_specs=[head_cols(c) for c in range(4)] + [head_cols(c) for c in range(6)],
        out_specs=pl.BlockSpec((DIL_TOK, width), lambda b, h, j: (b * nblk + j, h)),
        out_shape=jax.ShapeDtypeStruct((B * S, DIL_W), BF16),
        scratch_shapes=[ext(d) for _, d in DIL_CONFIG for _ in range(2)] + [tok_f32, tok_f32, tok_f32],
        compiler_params=_params(("arbitrary", "arbitrary", "arbitrary")),
        name="dilated_attention",
    )(proj, proj, proj, proj, kv, kv, kv, kv, kv, kv)


def kernel(x, mem, norm_a, w_in_a, w_out_a, norm_b, w_in_b, w_out_b, w_mem_kv, mem_norm_g, kv_norm_g, w_kv, final_norm_g):
    B, S, D = x.shape
    xr = x.reshape(B * S, D)
    mem_r = mem.reshape(B * mem.shape[1], D)

    w_in_a, w_out_a, w_in_b, w_out_b, w_mem_kv = (t.astype(BF16) for t in (w_in_a, w_out_a, w_in_b, w_out_b, w_mem_kv))
    w_kv = w_kv.astype(BF16)[None]
    norm_a = norm_a[:, None, :]
    norm_b = norm_b[:, None, :]
    mem_g = mem_norm_g[None, None, :]
    kv_g = kv_norm_g[None, None, :]

    dils = tuple(d for _, d in DIL_CONFIG)
    tiles_per_group = DIL_W // PROJ_TN
    group_copies = tuple(range(len(DIL_CONFIG)))
    in_b_copies = tuple(c for c in group_copies + (0,) * ((IN_W_B - len(DIL_CONFIG) * DIL_W) // DIL_W)
                        for _ in range(tiles_per_group))
    kv_copies = tuple(c for c in group_copies for _ in range(2 * tiles_per_group))

    kv = None
    for layer in range(DEPTH):
        mem_kv = norm_matmul(mem_r, mem_g, 0, w_mem_kv, layer, name=f"mem_kv_proj{layer}")
        if layer < N_A:
            proj = norm_matmul(xr, norm_a, layer, w_in_a, layer, name=f"in_proj_a{layer}")
            o_main = retention_gated(proj, B, S)
            o_mem = mem_attention_gated(proj, mem_kv, B, S, (IN_W_A - 2 * MEM_W) // MEM_W)
            xr = out_proj(o_main, o_mem, w_out_a, layer, xr, name=f"out_proj_a{layer}")
            if layer == N_A - 1:
                kv = norm_matmul(xr, kv_g, 0, w_kv, 0, name="shared_kv_proj", dils=dils, tile_copy=kv_copies)
        else:
            jb = layer - N_A
            proj = norm_matmul(xr, norm_b, jb, w_in_b, jb, name=f"in_proj_b{jb}", dils=dils, tile_copy=in_b_copies)
            o_main = dilated_gated(proj, kv, B, S)
            o_mem = mem_attention_gated(proj, mem_kv, B, S, (IN_W_B - 2 * MEM_W) // MEM_W)
            if layer < DEPTH - 1:
                xr = out_proj(o_main, o_mem, w_out_b, jb, xr, name=f"out_proj_b{jb}")
            else:
                xr = out_proj_norm(o_main, o_mem, w_out_b, jb, xr, final_norm_g[None, :], name="out_proj_final_norm")
    return xr.reshape(B, S, D)
```

```python
import functools
import math

import jax
import jax.numpy as jnp
from jax import lax
from jax.experimental import pallas as pl
from jax.experimental.pallas import tpu as pltpu

F32 = jnp.float32
BF16 = jnp.bfloat16

D_MODEL = 2048
DEPTH = 4
N_A = DEPTH // 2
MEM_LEN = 256
RET_HEADS = 8
RET_DK = D_MODEL // 16
RET_DV = 2 * RET_DK
RET_CHUNK = 128
RET_THETA_BASE = 10000.0
DIL_CONFIG = ((128, 1), (512, 4), (2048, 16))
DIL_HEADS = 16
DIL_DH = D_MODEL // 16
DIL_BLOCK = 128
MEM_HEADS = 4
MEM_DH = D_MODEL // 8
EPS = 1e-6

RET_QK_W = RET_HEADS * RET_DK
RET_V_W = RET_HEADS * RET_DV
DIL_W = DIL_HEADS * DIL_DH
MEM_W = MEM_HEADS * MEM_DH
IN_W_A = 2 * RET_QK_W + 2 * RET_V_W + 2 * MEM_W
IN_W_B = len(DIL_CONFIG) * DIL_W + DIL_W + 2 * MEM_W
KV_W = 2 * len(DIL_CONFIG) * DIL_W

VMEM_LIMIT_BYTES = 62 * 2**20
NORM_ROW_CHUNK = 64
PROJ_TM, PROJ_TN = 1024, 2048
OUT_TM, OUT_TN = 512, 2048
PERM_TOK = 256
RET_TQ = 1024
MEM_TQ = 2048
FINAL_TM = 512


def _params(semantics):
    return pltpu.CompilerParams(dimension_semantics=semantics, vmem_limit_bytes=VMEM_LIMIT_BYTES)


def _dot(a, b):
    return jnp.dot(a, b, preferred_element_type=F32)


def _dot_nt(a, b):
    return lax.dot_general(a, b, (((1,), (1,)), ((), ())), preferred_element_type=F32)


def _silu(x):
    return x * (1.0 / (1.0 + jnp.exp(-x)))


def _residue_order_matrix(dil):
    per_class = PERM_TOK // dil
    dst = lax.broadcasted_iota(jnp.int32, (PERM_TOK, PERM_TOK), 0)
    src = lax.broadcasted_iota(jnp.int32, (PERM_TOK, PERM_TOK), 1)
    wanted = (dst & (per_class - 1)) * dil + (dst >> int(math.log2(per_class)))
    return jnp.where(src == wanted, 1.0, 0.0).astype(BF16)


def _norm_matmul_kernel(x_ref, g_ref, w_ref, o_ref, xn_ref, *, dils, tile_copy):
    j = pl.program_id(1)

    @pl.when(j == 0)
    def _normalize():
        def body(c, carry):
            r = pl.multiple_of(c * NORM_ROW_CHUNK, NORM_ROW_CHUNK)
            xc = x_ref[pl.ds(r, NORM_ROW_CHUNK), :]
            ms = jnp.mean(xc * xc, axis=-1, keepdims=True)
            xn_ref[0, pl.ds(r, NORM_ROW_CHUNK), :] = (xc * lax.rsqrt(ms + EPS) * g_ref[...]).astype(BF16)
            return carry

        lax.fori_loop(0, x_ref.shape[0] // NORM_ROW_CHUNK, body, 0, unroll=2)
        for v in range(1, len(dils)):
            order = _residue_order_matrix(dils[v])
            for b in range(x_ref.shape[0] // PERM_TOK):
                rows = slice(b * PERM_TOK, (b + 1) * PERM_TOK)
                xn_ref[v, rows, :] = _dot(order, xn_ref[0, rows, :]).astype(BF16)

    copy = 0
    for t, v in enumerate(tile_copy):
        if v:
            copy = copy + v * (j == t).astype(jnp.int32)
    o_ref[...] = _dot(xn_ref[copy], w_ref[...]).astype(o_ref.dtype)


def norm_matmul(x, g, g_idx, w, w_idx, *, name, dils=(1,), tile_copy=None):
    M, K = x.shape
    N = w.shape[-1]
    tm, tn = min(PROJ_TM, M), min(PROJ_TN, N)
    tile_copy = tile_copy or (0,) * (N // tn)
    assert dils[0] == 1 and len(tile_copy) == N // tn and tm % PERM_TOK == 0
    return pl.pallas_call(
        functools.partial(_norm_matmul_kernel, dils=dils, tile_copy=tile_copy),
        grid=(M // tm, N // tn),
        in_specs=[
            pl.BlockSpec((tm, K), lambda i, j: (i, 0)),
            pl.BlockSpec((None, 1, K), lambda i, j: (g_idx, 0, 0)),
            pl.BlockSpec((None, K, tn), lambda i, j: (w_idx, 0, j)),
        ],
        out_specs=pl.BlockSpec((tm, tn), lambda i, j: (i, j)),
        out_shape=jax.ShapeDtypeStruct((M, N), BF16),
        scratch_shapes=[pltpu.VMEM((len(dils), tm, K), BF16)],
        compiler_params=_params(("parallel", "arbitrary")),
        name=name,
    )(x, g, w)


def _out_proj_kernel(om_ref, omem_ref, w_ref, x_ref, o_ref):
    k_main = om_ref.shape[1]
    acc = _dot(om_ref[...], w_ref[:k_main, :]) + _dot(omem_ref[...], w_ref[k_main:, :])
    o_ref[...] = x_ref[...] + acc


def out_proj(o_main, o_mem, w, layer, x, *, name):
    M, k_main = o_main.shape
    k_mem = o_mem.shape[1]
    N = w.shape[-1]
    tm, tn = OUT_TM, OUT_TN
    return pl.pallas_call(
        _out_proj_kernel,
        grid=(M // tm, N // tn),
        in_specs=[
            pl.BlockSpec((tm, k_main), lambda i, j: (i, 0)),
            pl.BlockSpec((tm, k_mem), lambda i, j: (i, 0)),
            pl.BlockSpec((None, k_main + k_mem, tn), lambda i, j: (layer, 0, j)),
            pl.BlockSpec((tm, tn), lambda i, j: (i, j)),
        ],
        out_specs=pl.BlockSpec((tm, tn), lambda i, j: (i, j)),
        out_shape=jax.ShapeDtypeStruct((M, N), F32),
        compiler_params=_params(("parallel", "parallel")),
        name=name,
    )(o_main, o_mem, w, x)


def _out_proj_norm_kernel(om_ref, omem_ref, w_ref, x_ref, g_ref, o_ref):
    k_main = om_ref.shape[1]
    o_ref[...] = x_ref[...] + (_dot(om_ref[...], w_ref[:k_main, :]) + _dot(omem_ref[...], w_ref[k_main:, :]))

    def body(c, carry):
        r = pl.multiple_of(c * NORM_ROW_CHUNK, NORM_ROW_CHUNK)
        xc = o_ref[pl.ds(r, NORM_ROW_CHUNK), :]
        ms = jnp.mean(xc * xc, axis=-1, keepdims=True)
        o_ref[pl.ds(r, NORM_ROW_CHUNK), :] = xc * lax.rsqrt(ms + EPS) * g_ref[...]
        return carry

    lax.fori_loop(0, o_ref.shape[0] // NORM_ROW_CHUNK, body, 0, unroll=2)


def out_proj_norm(o_main, o_mem, w, layer, x, g, *, name):
    M, k_main = o_main.shape
    k_mem = o_mem.shape[1]
    N = w.shape[-1]
    tm = FINAL_TM
    return pl.pallas_call(
        _out_proj_norm_kernel,
        grid=(M // tm,),
        in_specs=[
            pl.BlockSpec((tm, k_main), lambda i: (i, 0)),
            pl.BlockSpec((tm, k_mem), lambda i: (i, 0)),
            pl.BlockSpec((None, k_main + k_mem, N), lambda i: (layer, 0, 0)),
            pl.BlockSpec((tm, N), lambda i: (i, 0)),
            pl.BlockSpec((1, N), lambda i: (0, 0)),
        ],
        out_specs=pl.BlockSpec((tm, N), lambda i: (i, 0)),
        out_shape=jax.ShapeDtypeStruct((M, N), F32),
        compiler_params=_params(("parallel",)),
        name=name,
    )(o_main, o_mem, w, x, g)


def _retention_kernel(g_ref, q_ref, k_ref, v_ref, gate_ref, cos_ref, sin_ref, decay_ref, xi_ref, zeta_ref,
                      o_ref, state_ref):
    @pl.when(pl.program_id(1) == 0)
    def _reset():
        state_ref[...] = jnp.zeros_like(state_ref)

    def rotate(t, cos, sin):
        return t * cos + pltpu.roll(t, RET_DK // 2, axis=1) * sin

    for c in range(q_ref.shape[0] // RET_CHUNK):
        rows = pl.ds(c * RET_CHUNK, RET_CHUNK)
        cos = cos_ref[rows, :]
        sin = sin_ref[rows, :]
        for h in range(RET_HEADS):
            qk_cols = slice(h * RET_DK, (h + 1) * RET_DK)
            v_cols = slice(h * RET_DV, (h + 1) * RET_DV)
            q = rotate(q_ref[rows, qk_cols].astype(F32), cos, sin)
            k = rotate(k_ref[rows, qk_cols].astype(F32), cos, sin) * (RET_DK ** -0.5)
            v = v_ref[rows, v_cols]
            qb = q.astype(BF16)
            s = _dot_nt(qb, k.astype(BF16)) * decay_ref[h]
            state = state_ref[h]
            o = _dot(s.astype(BF16), v) + _dot(qb, state.astype(BF16)) * xi_ref[h]
            kz_t = (k * zeta_ref[h]).T.astype(BF16)
            state_ref[h] = g_ref[h] * state + _dot(kz_t, v)
            o = o * lax.rsqrt(jnp.mean(o * o, axis=-1, keepdims=True) + EPS)
            o_ref[rows, v_cols] = (o * _silu(gate_ref[rows, v_cols].astype(F32))).astype(o_ref.dtype)


def _rotary_tables(S):
    half = RET_DK // 2
    inv = 1.0 / (RET_THETA_BASE ** jnp.linspace(0.0, 1.0, half, dtype=F32))
    ang = jnp.arange(S, dtype=F32)[:, None] * inv[None, :]
    cos, sin = jnp.cos(ang), jnp.sin(ang)
    return jnp.concatenate([cos, cos], axis=-1), jnp.concatenate([-sin, sin], axis=-1)


def _decay_tables():
    H, C = RET_HEADS, RET_CHUNK
    log_g = jnp.log1p(-(2.0 ** (-5.0 - jnp.arange(H, dtype=F32))))
    pos = jnp.arange(C, dtype=F32)
    rel = pos[:, None] - pos[None, :]
    decay_in = jnp.where(rel >= 0, jnp.exp(jnp.maximum(rel, 0.0)[None] * log_g[:, None, None]), 0.0)
    xi = jnp.exp((pos[None, :] + 1.0) * log_g[:, None])
    zeta = jnp.exp((C - 1.0 - pos[None, :]) * log_g[:, None])
    g_chunk = jnp.exp(C * log_g)
    return g_chunk, decay_in, xi[:, :, None], zeta[:, :, None]


def retention_gated(proj, B, S):
    M = B * S
    tq = RET_TQ
    spb = S // tq
    g_chunk, decay_in, xi, zeta = _decay_tables()
    cos, sin = _rotary_tables(S)
    k_blk = 1
    v_blk = 2 * RET_QK_W // RET_V_W
    gate_blk = v_blk + 1
    row = lambda b, i: b * spb + i
    const3 = lambda b, i: (0, 0, 0)
    return pl.pallas_call(
        _retention_kernel,
        grid=(B, spb),
        in_specs=[
            pl.BlockSpec(memory_space=pltpu.SMEM),
            pl.BlockSpec((tq, RET_QK_W), lambda b, i: (row(b, i), 0)),
            pl.BlockSpec((tq, RET_QK_W), lambda b, i: (row(b, i), k_blk)),
            pl.BlockSpec((tq, RET_V_W), lambda b, i: (row(b, i), v_blk)),
            pl.BlockSpec((tq, RET_V_W), lambda b, i: (row(b, i), gate_blk)),
            pl.BlockSpec((tq, RET_DK), lambda b, i: (i, 0)),
            pl.BlockSpec((tq, RET_DK), lambda b, i: (i, 0)),
            pl.BlockSpec((RET_HEADS, RET_CHUNK, RET_CHUNK), const3),
            pl.BlockSpec((RET_HEADS, RET_CHUNK, 1), const3),
            pl.BlockSpec((RET_HEADS, RET_CHUNK, 1), const3),
        ],
        out_specs=pl.BlockSpec((tq, RET_V_W), lambda b, i: (row(b, i), 0)),
        out_shape=jax.ShapeDtypeStruct((M, RET_V_W), BF16),
        scratch_shapes=[pltpu.VMEM((RET_HEADS, RET_DK, RET_DV), F32)],
        compiler_params=_params(("parallel", "arbitrary")),
        name="retention",
    )(g_chunk, proj, proj, proj, proj, cos, sin, decay_in, xi, zeta)


def _mem_attn_kernel(q_ref, gate_ref, k_ref, v_ref, o_ref):
    for h in range(MEM_HEADS):
        cols = slice(h * MEM_DH, (h + 1) * MEM_DH)
        s = _dot_nt(q_ref[:, cols], k_ref[:, cols])
        p = jnp.exp2((s - jnp.max(s, axis=-1, keepdims=True)) * (MEM_DH ** -0.5 * math.log2(math.e)))
        l = jnp.sum(p, axis=-1, keepdims=True)
        o = _dot(p.astype(BF16), v_ref[:, cols]) * (1.0 / l)
        o_ref[:, cols] = (o * _silu(gate_ref[:, cols].astype(F32))).astype(o_ref.dtype)


def mem_attention_gated(proj, mem_kv, B, S, q_blk):
    M = B * S
    tq = MEM_TQ
    spb = S // tq
    row = lambda b, i: b * spb + i
    return pl.pallas_call(
        _mem_attn_kernel,
        grid=(B, spb),
        in_specs=[
            pl.BlockSpec((tq, MEM_W), lambda b, i: (row(b, i), q_blk)),
            pl.BlockSpec((tq, MEM_W), lambda b, i: (row(b, i), q_blk + 1)),
            pl.BlockSpec((MEM_LEN, MEM_W), lambda b, i: (b, 0)),
            pl.BlockSpec((MEM_LEN, MEM_W), lambda b, i: (b, 1)),
        ],
        out_specs=pl.BlockSpec((tq, MEM_W), lambda b, i: (row(b, i), 0)),
        out_shape=jax.ShapeDtypeStruct((M, MEM_W), BF16),
        compiler_params=_params(("parallel", "parallel")),
        name="mem_attention",
    )(proj, proj, mem_kv, mem_kv)


DIL_TOK = max(d for _, d in DIL_CONFIG) * DIL_BLOCK
DIL_KEYS = 2 * DIL_BLOCK


def _residue_rows(dil):
    per_class = DIL_TOK // dil
    return per_class, DIL_BLOCK + per_class


def _class_pieces(dil, r, pos0, count):
    per_perm = PERM_TOK // dil
    pieces = []
    while count:
        blk, off = divmod(pos0, per_perm)
        size = min(count, per_perm - off)
        pieces.append((blk * PERM_TOK + r * per_perm + off, size))
        pos0, count = pos0 + size, count - size
    return pieces


def _dilated_kernel(q0_ref, q1_ref, q2_ref, gate_ref, k0_ref, v0_ref, k1_ref, v1_ref, k2_ref, v2_ref, o_ref,
                    ke0_ref, ve0_ref, ke1_ref, ve1_ref, ke2_ref, ve2_ref, acc_ref, m_ref, l_ref):
    j = pl.program_id(2)
    q_refs = (q0_ref, q1_ref, q2_ref)
    kv_refs = ((k0_ref, v0_ref, ke0_ref, ve0_ref), (k1_ref, v1_ref, ke1_ref, ve1_ref), (k2_ref, v2_ref, ke2_ref, ve2_ref))

    for (k_ref, v_ref, ke_ref, ve_ref), (_, dil) in zip(kv_refs, DIL_CONFIG):
        per_class, seg = _residue_rows(dil)

        @pl.when(j == 0)
        def _zero_halo(ke_ref=ke_ref, ve_ref=ve_ref, dil=dil, seg=seg):
            for r in range(dil):
                ke_ref[r * seg:r * seg + DIL_BLOCK, :] = jnp.zeros((DIL_BLOCK, DIL_DH), BF16)
                ve_ref[r * seg:r * seg + DIL_BLOCK, :] = jnp.zeros((DIL_BLOCK, DIL_DH), BF16)

        @pl.when(j > 0)
        def _carry_halo(ke_ref=ke_ref, ve_ref=ve_ref, dil=dil, seg=seg, per_class=per_class):
            for r in range(dil):
                ke_ref[r * seg:r * seg + DIL_BLOCK, :] = ke_ref[r * seg + per_class:(r + 1) * seg, :]
                ve_ref[r * seg:r * seg + DIL_BLOCK, :] = ve_ref[r * seg + per_class:(r + 1) * seg, :]

        for r in range(dil):
            dst = r * seg + DIL_BLOCK
            for start, size in _class_pieces(dil, r, 0, per_class):
                ke_ref[dst:dst + size, :] = k_ref[start:start + size, :]
                ve_ref[dst:dst + size, :] = v_ref[start:start + size, :]
                dst += size

    key_lead = (lax.broadcasted_iota(jnp.int32, (DIL_BLOCK, DIL_BLOCK), 1)
                - lax.broadcasted_iota(jnp.int32, (DIL_BLOCK, DIL_BLOCK), 0))
    first_min_lead = jnp.where(j == 0, DIL_BLOCK, 0)
    exp2_scale = DIL_DH ** -0.5 * math.log2(math.e)

    def attend(q, ke_ref, ve_ref, key0, prev_min_lead, tok, first_group, last_group):
        kc = ke_ref[pl.ds(key0, DIL_KEYS), :]
        vc = ve_ref[pl.ds(key0, DIL_KEYS), :]
        s = _dot_nt(q, kc)
        s_prev = jnp.where(key_lead >= prev_min_lead, s[:, :DIL_BLOCK], -1e30)
        s_cur = jnp.where(key_lead <= 0, s[:, DIL_BLOCK:], -1e30)
        m_blk = jnp.max(jnp.maximum(s_prev, s_cur), axis=-1, keepdims=True)
        if first_group:
            m_new = jnp.broadcast_to(m_blk, (DIL_BLOCK, DIL_DH))
        else:
            m_old = m_ref[tok, :]
            m_new = jnp.maximum(m_old, m_blk)
            alpha = jnp.exp2((m_old - m_new) * exp2_scale)
        p_prev = jnp.exp2((s_prev - m_new) * exp2_scale)
        p_cur = jnp.exp2((s_cur - m_new) * exp2_scale)
        l = jnp.sum(p_prev + p_cur, axis=-1, keepdims=True)
        acc = _dot(jnp.concatenate([p_prev, p_cur], axis=1).astype(BF16), vc)
        if not first_group:
            l = alpha * l_ref[tok, :] + l
            acc = alpha * acc_ref[tok, :] + acc
        if last_group:
            o = acc * (1.0 / l) * _silu(gate_ref[tok, :].astype(F32))
            o_ref[tok, :] = o.astype(o_ref.dtype)
        else:
            l_ref[tok, :] = jnp.broadcast_to(l, (DIL_BLOCK, DIL_DH))
            acc_ref[tok, :] = acc
            m_ref[tok, :] = m_new

    groups = sorted(range(len(DIL_CONFIG)), key=lambda g: -DIL_CONFIG[g][1])
    for group in groups:
        dil = DIL_CONFIG[group][1]
        per_class, seg = _residue_rows(dil)
        _, _, ke_ref, ve_ref = kv_refs[group]
        for r in range(dil):
            for blk in range(per_class // DIL_BLOCK):
                pieces = _class_pieces(dil, r, blk * DIL_BLOCK, DIL_BLOCK)
                q = jnp.concatenate([q_refs[group][start:start + size, :] for start, size in pieces], axis=0)
                if dil == 1:
                    tok = pl.ds(blk * DIL_BLOCK, DIL_BLOCK)
                else:
                    tok = pl.ds(blk * DIL_BLOCK * dil + r, DIL_BLOCK, stride=dil)
                attend(q, ke_ref, ve_ref, r * seg + blk * DIL_BLOCK, first_min_lead if blk == 0 else 0, tok,
                       group == groups[0], group == groups[-1])


def dilated_gated(proj, kv, B, S):
    assert S % DIL_TOK == 0
    nblk = S // DIL_TOK
    heads_per_group = DIL_W // DIL_DH

    def head_cols(group_col):
        return pl.BlockSpec((DIL_TOK, DIL_DH), lambda b, h, j: (b * nblk + j, group_col * heads_per_group + h))

    ext = lambda dil: pltpu.VMEM((dil * _residue_rows(dil)[1], DIL_DH), BF16)
    tok_f32 = pltpu.VMEM((DIL_TOK, DIL_DH), F32)
    return pl.pallas_call(
        _dilated_kernel,
        grid=(B, DIL_HEADS, nblk),
        in_specs=[head_cols(c) for c in range(4)] + [head_cols(c) for c in range(6)],
        out_specs=pl.BlockSpec((DIL_TOK, DIL_DH), lambda b, h, j: (b * nblk + j, h)),
        out_shape=jax.ShapeDtypeStruct((B * S, DIL_W), BF16),
        scratch_shapes=[ext(d) for _, d in DIL_CONFIG for _ in range(2)] + [tok_f32, tok_f32, tok_f32],
        compiler_params=_params(("arbitrary", "arbitrary", "arbitrary")),
        name="dilated_attention",
    )(proj, proj, proj, proj, kv, kv, kv, kv, kv, kv)


def kernel(x, mem, norm_a, w_in_a, w_out_a, norm_b, w_in_b, w_out_b, w_mem_kv, mem_norm_g, kv_norm_g, w_kv, final_norm_g):
    B, S, D = x.shape
    xr = x.reshape(B * S, D)
    mem_r = mem.reshape(B * mem.shape[1], D)

    w_in_a, w_out_a, w_in_b, w_out_b, w_mem_kv = (t.astype(BF16) for t in (w_in_a, w_out_a, w_in_b, w_out_b, w_mem_kv))
    w_kv = w_kv.astype(BF16)[None]
    norm_a = norm_a[:, None, :]
    norm_b = norm_b[:, None, :]
    mem_g = mem_norm_g[None, None, :]
    kv_g = kv_norm_g[None, None, :]

    dils = tuple(d for _, d in DIL_CONFIG)
    tiles_per_group = DIL_W // PROJ_TN
    group_copies = tuple(range(len(DIL_CONFIG)))
    in_b_copies = tuple(c for c in group_copies + (0,) * ((IN_W_B - len(DIL_CONFIG) * DIL_W) // DIL_W)
                        for _ in range(tiles_per_group))
    kv_copies = tuple(c for c in group_copies for _ in range(2 * tiles_per_group))

    kv = None
    for layer in range(DEPTH):
        mem_kv = norm_matmul(mem_r, mem_g, 0, w_mem_kv, layer, name=f"mem_kv_proj{layer}")
        if layer < N_A:
            proj = norm_matmul(xr, norm_a, layer, w_in_a, layer, name=f"in_proj_a{layer}")
            o_main = retention_gated(proj, B, S)
            o_mem = mem_attention_gated(proj, mem_kv, B, S, (IN_W_A - 2 * MEM_W) // MEM_W)
            xr = out_proj(o_main, o_mem, w_out_a, layer, xr, name=f"out_proj_a{layer}")
            if layer == N_A - 1:
                kv = norm_matmul(xr, kv_g, 0, w_kv, 0, name="shared_kv_proj", dils=dils, tile_copy=kv_copies)
        else:
            jb = layer - N_A
            proj = norm_matmul(xr, norm_b, jb, w_in_b, jb, name=f"in_proj_b{jb}", dils=dils, tile_copy=in_b_copies)
            o_main = dilated_gated(proj, kv, B, S)
            o_mem = mem_attention_gated(proj, mem_kv, B, S, (IN_W_B - 2 * MEM_W) // MEM_W)
            if layer < DEPTH - 1:
                xr = out_proj(o_main, o_mem, w_out_b, jb, xr, name=f"out_proj_b{jb}")
            else:
                xr = out_proj_norm(o_main, o_mem, w_out_b, jb, xr, final_norm_g[None, :], name="out_proj_final_norm")
    return xr.reshape(B, S, D)
```

```python
import functools
import math

import jax
import jax.numpy as jnp
from jax import lax
from jax.experimental import pallas as pl
from jax.experimental.pallas import tpu as pltpu

F32 = jnp.float32
BF16 = jnp.bfloat16

D_MODEL = 2048
DEPTH = 4
N_A = DEPTH // 2
MEM_LEN = 256
RET_HEADS = 8
RET_DK = D_MODEL // 16
RET_DV = 2 * RET_DK
RET_CHUNK = 128
RET_THETA_BASE = 10000.0
DIL_CONFIG = ((128, 1), (512, 4), (2048, 16))
DIL_HEADS = 16
DIL_DH = D_MODEL // 16
DIL_BLOCK = 128
MEM_HEADS = 4
MEM_DH = D_MODEL // 8
EPS = 1e-6

RET_QK_W = RET_HEADS * RET_DK
RET_V_W = RET_HEADS * RET_DV
DIL_W = DIL_HEADS * DIL_DH
MEM_W = MEM_HEADS * MEM_DH
IN_W_A = 2 * RET_QK_W + 2 * RET_V_W + 2 * MEM_W
IN_W_B = len(DIL_CONFIG) * DIL_W + DIL_W + 2 * MEM_W
KV_W = 2 * len(DIL_CONFIG) * DIL_W

VMEM_LIMIT_BYTES = 62 * 2**20
NORM_ROW_CHUNK = 64
PROJ_TM, PROJ_TN = 1024, 2048
OUT_TM, OUT_TN = 512, 2048
PERM_TOK = 256
RET_TQ = 1024
MEM_TQ = 2048
FINAL_TM = 512


def _params(semantics):
    return pltpu.CompilerParams(dimension_semantics=semantics, vmem_limit_bytes=VMEM_LIMIT_BYTES)


def _dot(a, b):
    return jnp.dot(a, b, preferred_element_type=F32)


def _dot_nt(a, b):
    return lax.dot_general(a, b, (((1,), (1,)), ((), ())), preferred_element_type=F32)


def _silu(x):
    return x * (1.0 / (1.0 + jnp.exp(-x)))


def _residue_order_matrix(dil):
    per_class = PERM_TOK // dil
    dst = lax.broadcasted_iota(jnp.int32, (PERM_TOK, PERM_TOK), 0)
    src = lax.broadcasted_iota(jnp.int32, (PERM_TOK, PERM_TOK), 1)
    wanted = (dst & (per_class - 1)) * dil + (dst >> int(math.log2(per_class)))
    return jnp.where(src == wanted, 1.0, 0.0).astype(BF16)


def _norm_matmul_kernel(x_ref, g_ref, w_ref, o_ref, xn_ref, *, dils, tile_copy):
    j = pl.program_id(1)

    @pl.when(j == 0)
    def _normalize():
        def body(c, carry):
            r = pl.multiple_of(c * NORM_ROW_CHUNK, NORM_ROW_CHUNK)
            xc = x_ref[pl.ds(r, NORM_ROW_CHUNK), :]
            ms = jnp.mean(xc * xc, axis=-1, keepdims=True)
            xn_ref[0, pl.ds(r, NORM_ROW_CHUNK), :] = (xc * lax.rsqrt(ms + EPS) * g_ref[...]).astype(BF16)
            return carry

        lax.fori_loop(0, x_ref.shape[0] // NORM_ROW_CHUNK, body, 0, unroll=4)
        for v in range(1, len(dils)):
            order = _residue_order_matrix(dils[v])
            for b in range(x_ref.shape[0] // PERM_TOK):
                rows = slice(b * PERM_TOK, (b + 1) * PERM_TOK)
                xn_ref[v, rows, :] = _dot(order, xn_ref[0, rows, :]).astype(BF16)

    copy = 0
    for t, v in enumerate(tile_copy):
        if v:
            copy = copy + v * (j == t).astype(jnp.int32)
    o_ref[...] = _dot(xn_ref[copy], w_ref[...]).astype(o_ref.dtype)


def norm_matmul(x, g, g_idx, w, w_idx, *, name, dils=(1,), tile_copy=None):
    M, K = x.shape
    N = w.shape[-1]
    tm, tn = min(PROJ_TM, M), min(PROJ_TN, N)
    tile_copy = tile_copy or (0,) * (N // tn)
    assert dils[0] == 1 and len(tile_copy) == N // tn and tm % PERM_TOK == 0
    return pl.pallas_call(
        functools.partial(_norm_matmul_kernel, dils=dils, tile_copy=tile_copy),
        grid=(M // tm, N // tn),
        in_specs=[
            pl.BlockSpec((tm, K), lambda i, j: (i, 0)),
            pl.BlockSpec((None, 1, K), lambda i, j: (g_idx, 0, 0)),
            pl.BlockSpec((None, K, tn), lambda i, j: (w_idx, 0, j)),
        ],
        out_specs=pl.BlockSpec((tm, tn), lambda i, j: (i, j)),
        out_shape=jax.ShapeDtypeStruct((M, N), BF16),
        scratch_shapes=[pltpu.VMEM((len(dils), tm, K), BF16)],
        compiler_params=_params(("parallel", "arbitrary")),
        name=name,
    )(x, g, w)


def _out_proj_kernel(om_ref, omem_ref, w_ref, x_ref, o_ref):
    k_main = om_ref.shape[1]
    acc = _dot(om_ref[...], w_ref[:k_main, :]) + _dot(omem_ref[...], w_ref[k_main:, :])
    o_ref[...] = x_ref[...] + acc


def out_proj(o_main, o_mem, w, layer, x, *, name):
    M, k_main = o_main.shape
    k_mem = o_mem.shape[1]
    N = w.shape[-1]
    tm, tn = OUT_TM, OUT_TN
    return pl.pallas_call(
        _out_proj_kernel,
        grid=(M // tm, N // tn),
        in_specs=[
            pl.BlockSpec((tm, k_main), lambda i, j: (i, 0)),
            pl.BlockSpec((tm, k_mem), lambda i, j: (i, 0)),
            pl.BlockSpec((None, k_main + k_mem, tn), lambda i, j: (layer, 0, j)),
            pl.BlockSpec((tm, tn), lambda i, j: (i, j)),
        ],
        out_specs=pl.BlockSpec((tm, tn), lambda i, j: (i, j)),
        out_shape=jax.ShapeDtypeStruct((M, N), F32),
        compiler_params=_params(("parallel", "parallel")),
        name=name,
    )(o_main, o_mem, w, x)


def _out_proj_norm_kernel(om_ref, omem_ref, w_ref, x_ref, g_ref, o_ref):
    k_main = om_ref.shape[1]
    o_ref[...] = x_ref[...] + (_dot(om_ref[...], w_ref[:k_main, :]) + _dot(omem_ref[...], w_ref[k_main:, :]))

    def body(c, carry):
        r = pl.multiple_of(c * NORM_ROW_CHUNK, NORM_ROW_CHUNK)
        xc = o_ref[pl.ds(r, NORM_ROW_CHUNK), :]
        ms = jnp.mean(xc * xc, axis=-1, keepdims=True)
        o_ref[pl.ds(r, NORM_ROW_CHUNK), :] = xc * lax.rsqrt(ms + EPS) * g_ref[...]
        return carry

    lax.fori_loop(0, o_ref.shape[0] // NORM_ROW_CHUNK, body, 0, unroll=2)


def out_proj_norm(o_main, o_mem, w, layer, x, g, *, name):
    M, k_main = o_main.shape
    k_mem = o_mem.shape[1]
    N = w.shape[-1]
    tm = FINAL_TM
    return pl.pallas_call(
        _out_proj_norm_kernel,
        grid=(M // tm,),
        in_specs=[
            pl.BlockSpec((tm, k_main), lambda i: (i, 0)),
            pl.BlockSpec((tm, k_mem), lambda i: (i, 0)),
            pl.BlockSpec((None, k_main + k_mem, N), lambda i: (layer, 0, 0)),
            pl.BlockSpec((tm, N), lambda i: (i, 0)),
            pl.BlockSpec((1, N), lambda i: (0, 0)),
        ],
        out_specs=pl.BlockSpec((tm, N), lambda i: (i, 0)),
        out_shape=jax.ShapeDtypeStruct((M, N), F32),
        compiler_params=_params(("parallel",)),
        name=name,
    )(o_main, o_mem, w, x, g)


def _retention_kernel(g_ref, q_ref, k_ref, v_ref, gate_ref, cos_ref, sin_ref, decay_ref, xi_ref, zeta_ref,
                      o_ref, state_ref):
    @pl.when(pl.program_id(1) == 0)
    def _reset():
        state_ref[...] = jnp.zeros_like(state_ref)

    def rotate(t, cos, sin):
        return t * cos + pltpu.roll(t, RET_DK // 2, axis=1) * sin

    for c in range(q_ref.shape[0] // RET_CHUNK):
        rows = pl.ds(c * RET_CHUNK, RET_CHUNK)
        cos = cos_ref[rows, :]
        sin = sin_ref[rows, :]
        for h in range(RET_HEADS):
            qk_cols = slice(h * RET_DK, (h + 1) * RET_DK)
            v_cols = slice(h * RET_DV, (h + 1) * RET_DV)
            q = rotate(q_ref[rows, qk_cols].astype(F32), cos, sin)
            k = rotate(k_ref[rows, qk_cols].astype(F32), cos, sin) * (RET_DK ** -0.5)
            v = v_ref[rows, v_cols]
            qb = q.astype(BF16)
            s = _dot_nt(qb, k.astype(BF16)) * decay_ref[h]
            state = state_ref[h]
            o = _dot(s.astype(BF16), v) + _dot(qb, state.astype(BF16)) * xi_ref[h]
            kz_t = (k * zeta_ref[h]).T.astype(BF16)
            state_ref[h] = g_ref[h] * state + _dot(kz_t, v)
            o = o * lax.rsqrt(jnp.mean(o * o, axis=-1, keepdims=True) + EPS)
            o_ref[rows, v_cols] = (o * _silu(gate_ref[rows, v_cols].astype(F32))).astype(o_ref.dtype)


def _rotary_tables(S):
    half = RET_DK // 2
    inv = 1.0 / (RET_THETA_BASE ** jnp.linspace(0.0, 1.0, half, dtype=F32))
    ang = jnp.arange(S, dtype=F32)[:, None] * inv[None, :]
    cos, sin = jnp.cos(ang), jnp.sin(ang)
    return jnp.concatenate([cos, cos], axis=-1), jnp.concatenate([-sin, sin], axis=-1)


def _decay_tables():
    H, C = RET_HEADS, RET_CHUNK
    log_g = jnp.log1p(-(2.0 ** (-5.0 - jnp.arange(H, dtype=F32))))
    pos = jnp.arange(C, dtype=F32)
    rel = pos[:, None] - pos[None, :]
    decay_in = jnp.where(rel >= 0, jnp.exp(jnp.maximum(rel, 0.0)[None] * log_g[:, None, None]), 0.0)
    xi = jnp.exp((pos[None, :] + 1.0) * log_g[:, None])
    zeta = jnp.exp((C - 1.0 - pos[None, :]) * log_g[:, None])
    g_chunk = jnp.exp(C * log_g)
    return g_chunk, decay_in, xi[:, :, None], zeta[:, :, None]


def retention_gated(proj, B, S):
    M = B * S
    tq = RET_TQ
    spb = S // tq
    g_chunk, decay_in, xi, zeta = _decay_tables()
    cos, sin = _rotary_tables(S)
    k_blk = 1
    v_blk = 2 * RET_QK_W // RET_V_W
    gate_blk = v_blk + 1
    row = lambda b, i: b * spb + i
    const3 = lambda b, i: (0, 0, 0)
    return pl.pallas_call(
        _retention_kernel,
        grid=(B, spb),
        in_specs=[
            pl.BlockSpec(memory_space=pltpu.SMEM),
            pl.BlockSpec((tq, RET_QK_W), lambda b, i: (row(b, i), 0)),
            pl.BlockSpec((tq, RET_QK_W), lambda b, i: (row(b, i), k_blk)),
            pl.BlockSpec((tq, RET_V_W), lambda b, i: (row(b, i), v_blk)),
            pl.BlockSpec((tq, RET_V_W), lambda b, i: (row(b, i), gate_blk)),
            pl.BlockSpec((tq, RET_DK), lambda b, i: (i, 0)),
            pl.BlockSpec((tq, RET_DK), lambda b, i: (i, 0)),
            pl.BlockSpec((RET_HEADS, RET_CHUNK, RET_CHUNK), const3),
            pl.BlockSpec((RET_HEADS, RET_CHUNK, 1), const3),
            pl.BlockSpec((RET_HEADS, RET_CHUNK, 1), const3),
        ],
        out_specs=pl.BlockSpec((tq, RET_V_W), lambda b, i: (row(b, i), 0)),
        out_shape=jax.ShapeDtypeStruct((M, RET_V_W), BF16),
        scratch_shapes=[pltpu.VMEM((RET_HEADS, RET_DK, RET_DV), F32)],
        compiler_params=_params(("parallel", "arbitrary")),
        name="retention",
    )(g_chunk, proj, proj, proj, proj, cos, sin, decay_in, xi, zeta)


def _mem_attn_kernel(q_ref, gate_ref, k_ref, v_ref, o_ref):
    for h in range(MEM_HEADS):
        cols = slice(h * MEM_DH, (h + 1) * MEM_DH)
        s = _dot_nt(q_ref[:, cols], k_ref[:, cols])
        p = jnp.exp2((s - jnp.max(s, axis=-1, keepdims=True)) * (MEM_DH ** -0.5 * math.log2(math.e)))
        l = jnp.sum(p, axis=-1, keepdims=True)
        o = _dot(p.astype(BF16), v_ref[:, cols]) * (1.0 / l)
        o_ref[:, cols] = (o * _silu(gate_ref[:, cols].astype(F32))).astype(o_ref.dtype)


def mem_attention_gated(proj, mem_kv, B, S, q_blk):
    M = B * S
    tq = MEM_TQ
    spb = S // tq
    row = lambda b, i: b * spb + i
    return pl.pallas_call(
        _mem_attn_kernel,
        grid=(B, spb),
        in_specs=[
            pl.BlockSpec((tq, MEM_W), lambda b, i: (row(b, i), q_blk)),
            pl.BlockSpec((tq, MEM_W), lambda b, i: (row(b, i), q_blk + 1)),
            pl.BlockSpec((MEM_LEN, MEM_W), lambda b, i: (b, 0)),
            pl.BlockSpec((MEM_LEN, MEM_W), lambda b, i: (b, 1)),
        ],
        out_specs=pl.BlockSpec((tq, MEM_W), lambda b, i: (row(b, i), 0)),
        out_shape=jax.ShapeDtypeStruct((M, MEM_W), BF16),
        compiler_params=_params(("parallel", "parallel")),
        name="mem_attention",
    )(proj, proj, mem_kv, mem_kv)


DIL_TOK = max(d for _, d in DIL_CONFIG) * DIL_BLOCK
DIL_KEYS = 2 * DIL_BLOCK


def _residue_rows(dil):
    per_class = DIL_TOK // dil
    return per_class, DIL_BLOCK + per_class


def _class_pieces(dil, r, pos0, count):
    per_perm = PERM_TOK // dil
    pieces = []
    while count:
        blk, off = divmod(pos0, per_perm)
        size = min(count, per_perm - off)
        pieces.append((blk * PERM_TOK + r * per_perm + off, size))
        pos0, count = pos0 + size, count - size
    return pieces


def _dilated_kernel(q0_ref, q1_ref, q2_ref, gate_ref, k0_ref, v0_ref, k1_ref, v1_ref, k2_ref, v2_ref, o_ref,
                    ke0_ref, ve0_ref, ke1_ref, ve1_ref, ke2_ref, ve2_ref, acc_ref, m_ref, l_ref):
    j = pl.program_id(2)
    q_refs = (q0_ref, q1_ref, q2_ref)
    kv_refs = ((k0_ref, v0_ref, ke0_ref, ve0_ref), (k1_ref, v1_ref, ke1_ref, ve1_ref), (k2_ref, v2_ref, ke2_ref, ve2_ref))

    for (k_ref, v_ref, ke_ref, ve_ref), (_, dil) in zip(kv_refs, DIL_CONFIG):
        per_class, seg = _residue_rows(dil)

        @pl.when(j == 0)
        def _zero_halo(ke_ref=ke_ref, ve_ref=ve_ref, dil=dil, seg=seg):
            for r in range(dil):
                ke_ref[r * seg:r * seg + DIL_BLOCK, :] = jnp.zeros((DIL_BLOCK, DIL_DH), BF16)
                ve_ref[r * seg:r * seg + DIL_BLOCK, :] = jnp.zeros((DIL_BLOCK, DIL_DH), BF16)

        @pl.when(j > 0)
        def _carry_halo(ke_ref=ke_ref, ve_ref=ve_ref, dil=dil, seg=seg, per_class=per_class):
            for r in range(dil):
                ke_ref[r * seg:r * seg + DIL_BLOCK, :] = ke_ref[r * seg + per_class:(r + 1) * seg, :]
                ve_ref[r * seg:r * seg + DIL_BLOCK, :] = ve_ref[r * seg + per_class:(r + 1) * seg, :]

        for r in range(dil):
            dst = r * seg + DIL_BLOCK
            for start, size in _class_pieces(dil, r, 0, per_class):
                ke_ref[dst:dst + size, :] = k_ref[start:start + size, :]
                ve_ref[dst:dst + size, :] = v_ref[start:start + size, :]
                dst += size

    key_lead = (lax.broadcasted_iota(jnp.int32, (DIL_BLOCK, DIL_BLOCK), 1)
                - lax.broadcasted_iota(jnp.int32, (DIL_BLOCK, DIL_BLOCK), 0))
    first_min_lead = jnp.where(j == 0, DIL_BLOCK, 0)
    exp2_scale = DIL_DH ** -0.5 * math.log2(math.e)

    def attend(q, ke_ref, ve_ref, key0, prev_min_lead, tok, first_group, last_group):
        kc = ke_ref[pl.ds(key0, DIL_KEYS), :]
        vc = ve_ref[pl.ds(key0, DIL_KEYS), :]
        s = _dot_nt(q, kc)
        s_prev = jnp.where(key_lead >= prev_min_lead, s[:, :DIL_BLOCK], -1e30)
        s_cur = jnp.where(key_lead <= 0, s[:, DIL_BLOCK:], -1e30)
        m_blk = jnp.max(jnp.maximum(s_prev, s_cur), axis=-1, keepdims=True)
        if first_group:
            m_new = jnp.broadcast_to(m_blk, (DIL_BLOCK, DIL_DH))
        else:
            m_old = m_ref[tok, :]
            m_new = jnp.maximum(m_old, m_blk)
            alpha = jnp.exp2((m_old - m_new) * exp2_scale)
        p_prev = jnp.exp2((s_prev - m_new) * exp2_scale)
        p_cur = jnp.exp2((s_cur - m_new) * exp2_scale)
        l = jnp.sum(p_prev + p_cur, axis=-1, keepdims=True)
        acc = _dot(jnp.concatenate([p_prev, p_cur], axis=1).astype(BF16), vc)
        if not first_group:
            l = alpha * l_ref[tok, :] + l
            acc = alpha * acc_ref[tok, :] + acc
        if last_group:
            o = acc * (1.0 / l) * _silu(gate_ref[tok, :].astype(F32))
            o_ref[tok, :] = o.astype(o_ref.dtype)
        else:
            l_ref[tok, :] = jnp.broadcast_to(l, (DIL_BLOCK, DIL_DH))
            acc_ref[tok, :] = acc
            m_ref[tok, :] = m_new

    groups = sorted(range(len(DIL_CONFIG)), key=lambda g: -DIL_CONFIG[g][1])
    for group in groups:
        dil = DIL_CONFIG[group][1]
        per_class, seg = _residue_rows(dil)
        _, _, ke_ref, ve_ref = kv_refs[group]
        for r in range(dil):
            for blk in range(per_class // DIL_BLOCK):
                pieces = _class_pieces(dil, r, blk * DIL_BLOCK, DIL_BLOCK)
                q = jnp.concatenate([q_refs[group][start:start + size, :] for start, size in pieces], axis=0)
                if dil == 1:
                    tok = pl.ds(blk * DIL_BLOCK, DIL_BLOCK)
                else:
                    tok = pl.ds(blk * DIL_BLOCK * dil + r, DIL_BLOCK, stride=dil)
                attend(q, ke_ref, ve_ref, r * seg + blk * DIL_BLOCK, first_min_lead if blk == 0 else 0, tok,
                       group == groups[0], group == groups[-1])


def dilated_gated(proj, kv, B, S):
    assert S % DIL_TOK == 0
    nblk = S // DIL_TOK
    heads_per_group = DIL_W // DIL_DH

    def head_cols(group_col):
        return pl.BlockSpec((DIL_TOK, DIL_DH), lambda b, h, j: (b * nblk + j, group_col * heads_per_group + h))

    ext = lambda dil: pltpu.VMEM((dil * _residue_rows(dil)[1], DIL_DH), BF16)
    tok_f32 = pltpu.VMEM((DIL_TOK, DIL_DH), F32)
    return pl.pallas_call(
        _dilated_kernel,
        grid=(B, DIL_HEADS, nblk),
        in_specs=[head_cols(c) for c in range(4)] + [head_cols(c) for c in range(6)],
        out_specs=pl.BlockSpec((DIL_TOK, DIL_DH), lambda b, h, j: (b * nblk + j, h)),
        out_shape=jax.ShapeDtypeStruct((B * S, DIL_W), BF16),
        scratch_shapes=[ext(d) for _, d in DIL_CONFIG for _ in range(2)] + [tok_f32, tok_f32, tok_f32],
        compiler_params=_params(("arbitrary", "arbitrary", "arbitrary")),
        name="dilated_attention",
    )(proj, proj, proj, proj, kv, kv, kv, kv, kv, kv)


def kernel(x, mem, norm_a, w_in_a, w_out_a, norm_b, w_in_b, w_out_b, w_mem_kv, mem_norm_g, kv_norm_g, w_kv, final_norm_g):
    B, S, D = x.shape
    xr = x.reshape(B * S, D)
    mem_r = mem.reshape(B * mem.shape[1], D)

    w_in_a, w_out_a, w_in_b, w_out_b, w_mem_kv = (t.astype(BF16) for t in (w_in_a, w_out_a, w_in_b, w_out_b, w_mem_kv))
    w_kv = w_kv.astype(BF16)[None]
    norm_a = norm_a[:, None, :]
    norm_b = norm_b[:, None, :]
    mem_g = mem_norm_g[None, None, :]
    kv_g = kv_norm_g[None, None, :]

    dils = tuple(d for _, d in DIL_CONFIG)
    tiles_per_group = DIL_W // PROJ_TN
    group_copies = tuple(range(len(DIL_CONFIG)))
    in_b_copies = tuple(c for c in group_copies + (0,) * ((IN_W_B - len(DIL_CONFIG) * DIL_W) // DIL_W)
                        for _ in range(tiles_per_group))
    kv_copies = tuple(c for c in group_copies for _ in range(2 * tiles_per_group))

    kv = None
    for layer in range(DEPTH):
        mem_kv = norm_matmul(mem_r, mem_g, 0, w_mem_kv, layer, name=f"mem_kv_proj{layer}")
        if layer < N_A:
            proj = norm_matmul(xr, norm_a, layer, w_in_a, layer, name=f"in_proj_a{layer}")
            o_main = retention_gated(proj, B, S)
            o_mem = mem_attention_gated(proj, mem_kv, B, S, (IN_W_A - 2 * MEM_W) // MEM_W)
            xr = out_proj(o_main, o_mem, w_out_a, layer, xr, name=f"out_proj_a{layer}")
            if layer == N_A - 1:
                kv = norm_matmul(xr, kv_g, 0, w_kv, 0, name="shared_kv_proj", dils=dils, tile_copy=kv_copies)
        else:
            jb = layer - N_A
            proj = norm_matmul(xr, norm_b, jb, w_in_b, jb, name=f"in_proj_b{jb}", dils=dils, tile_copy=in_b_copies)
            o_main = dilated_gated(proj, kv, B, S)
            o_mem = mem_attention_gated(proj, mem_kv, B, S, (IN_W_B - 2 * MEM_W) // MEM_W)
            if layer < DEPTH - 1:
                xr = out_proj(o_main, o_mem, w_out_b, jb, xr, name=f"out_proj_b{jb}")
            else:
                xr = out_proj_norm(o_main, o_mem, w_out_b, jb, xr, final_norm_g[None, :], name="out_proj_final_norm")
    return xr.reshape(B, S, D)
```

```python
import functools
import math

import jax
import jax.numpy as jnp
from jax import lax
from jax.experimental import pallas as pl
from jax.experimental.pallas import tpu as pltpu

F32 = jnp.float32
BF16 = jnp.bfloat16

D_MODEL = 2048
DEPTH = 4
N_A = DEPTH // 2
MEM_LEN = 256
RET_HEADS = 8
RET_DK = D_MODEL // 16
RET_DV = 2 * RET_DK
RET_CHUNK = 128
RET_THETA_BASE = 10000.0
DIL_CONFIG = ((128, 1), (512, 4), (2048, 16))
DIL_HEADS = 16
DIL_DH = D_MODEL // 16
DIL_BLOCK = 128
MEM_HEADS = 4
MEM_DH = D_MODEL // 8
EPS = 1e-6

RET_QK_W = RET_HEADS * RET_DK
RET_V_W = RET_HEADS * RET_DV
DIL_W = DIL_HEADS * DIL_DH
MEM_W = MEM_HEADS * MEM_DH
IN_W_A = 2 * RET_QK_W + 2 * RET_V_W + 2 * MEM_W
IN_W_B = len(DIL_CONFIG) * DIL_W + DIL_W + 2 * MEM_W
KV_W = 2 * len(DIL_CONFIG) * DIL_W

VMEM_LIMIT_BYTES = 62 * 2**20
NORM_ROW_CHUNK = 64
PROJ_TM, PROJ_TN = 1024, 2048
OUT_TM, OUT_TN = 512, 2048
PERM_TOK = 256
RET_TQ = 1024
MEM_TQ = 2048
FINAL_TM = 512


def _params(semantics):
    return pltpu.CompilerParams(dimension_semantics=semantics, vmem_limit_bytes=VMEM_LIMIT_BYTES)


def _dot(a, b):
    return jnp.dot(a, b, preferred_element_type=F32)


def _dot_nt(a, b):
    return lax.dot_general(a, b, (((1,), (1,)), ((), ())), preferred_element_type=F32)


def _silu(x):
    return x * (1.0 / (1.0 + jnp.exp(-x)))


def _residue_order_matrix(dil):
    per_class = PERM_TOK // dil
    dst = lax.broadcasted_iota(jnp.int32, (PERM_TOK, PERM_TOK), 0)
    src = lax.broadcasted_iota(jnp.int32, (PERM_TOK, PERM_TOK), 1)
    wanted = (dst & (per_class - 1)) * dil + (dst >> int(math.log2(per_class)))
    return jnp.where(src == wanted, 1.0, 0.0).astype(BF16)


def _norm_matmul_kernel(x_ref, g_ref, w_ref, o_ref, xn_ref, *, dils, tile_copy):
    j = pl.program_id(1)

    @pl.when(j == 0)
    def _normalize():
        def body(c, carry):
            r = pl.multiple_of(c * NORM_ROW_CHUNK, NORM_ROW_CHUNK)
            xc = x_ref[pl.ds(r, NORM_ROW_CHUNK), :]
            ms = jnp.mean(xc * xc, axis=-1, keepdims=True)
            xn_ref[0, pl.ds(r, NORM_ROW_CHUNK), :] = (xc * lax.rsqrt(ms + EPS) * g_ref[...]).astype(BF16)
            return carry

        lax.fori_loop(0, x_ref.shape[0] // NORM_ROW_CHUNK, body, 0, unroll=4)
        for v in range(1, len(dils)):
            order = _residue_order_matrix(dils[v])
            for b in range(x_ref.shape[0] // PERM_TOK):
                rows = slice(b * PERM_TOK, (b + 1) * PERM_TOK)
                xn_ref[v, rows, :] = _dot(order, xn_ref[0, rows, :]).astype(BF16)

    copy = 0
    for t, v in enumerate(tile_copy):
        if v:
            copy = copy + v * (j == t).astype(jnp.int32)
    o_ref[...] = _dot(xn_ref[copy], w_ref[...]).astype(o_ref.dtype)


def norm_matmul(x, g, g_idx, w, w_idx, *, name, dils=(1,), tile_copy=None):
    M, K = x.shape
    N = w.shape[-1]
    tm, tn = min(PROJ_TM, M), min(PROJ_TN, N)
    tile_copy = tile_copy or (0,) * (N // tn)
    assert dils[0] == 1 and len(tile_copy) == N // tn and tm % PERM_TOK == 0
    return pl.pallas_call(
        functools.partial(_norm_matmul_kernel, dils=dils, tile_copy=tile_copy),
        grid=(M // tm, N // tn),
        in_specs=[
            pl.BlockSpec((tm, K), lambda i, j: (i, 0)),
            pl.BlockSpec((None, 1, K), lambda i, j: (g_idx, 0, 0)),
            pl.BlockSpec((None, K, tn), lambda i, j: (w_idx, 0, j)),
        ],
        out_specs=pl.BlockSpec((tm, tn), lambda i, j: (i, j)),
        out_shape=jax.ShapeDtypeStruct((M, N), BF16),
        scratch_shapes=[pltpu.VMEM((len(dils), tm, K), BF16)],
        compiler_params=_params(("parallel", "arbitrary")),
        name=name,
    )(x, g, w)


def _out_proj_kernel(om_ref, omem_ref, w_ref, x_ref, o_ref):
    k_main = om_ref.shape[1]
    acc = _dot(om_ref[...], w_ref[:k_main, :]) + _dot(omem_ref[...], w_ref[k_main:, :])
    o_ref[...] = x_ref[...] + acc


def out_proj(o_main, o_mem, w, layer, x, *, name):
    M, k_main = o_main.shape
    k_mem = o_mem.shape[1]
    N = w.shape[-1]
    tm, tn = OUT_TM, OUT_TN
    return pl.pallas_call(
        _out_proj_kernel,
        grid=(M // tm, N // tn),
        in_specs=[
            pl.BlockSpec((tm, k_main), lambda i, j: (i, 0)),
            pl.BlockSpec((tm, k_mem), lambda i, j: (i, 0)),
            pl.BlockSpec((None, k_main + k_mem, tn), lambda i, j: (layer, 0, j)),
            pl.BlockSpec((tm, tn), lambda i, j: (i, j)),
        ],
        out_specs=pl.BlockSpec((tm, tn), lambda i, j: (i, j)),
        out_shape=jax.ShapeDtypeStruct((M, N), F32),
        compiler_params=_params(("parallel", "parallel")),
        name=name,
    )(o_main, o_mem, w, x)


def _out_proj_norm_kernel(om_ref, omem_ref, w_ref, x_ref, g_ref, o_ref):
    k_main = om_ref.shape[1]
    o_ref[...] = x_ref[...] + (_dot(om_ref[...], w_ref[:k_main, :]) + _dot(omem_ref[...], w_ref[k_main:, :]))

    def body(c, carry):
        r = pl.multiple_of(c * NORM_ROW_CHUNK, NORM_ROW_CHUNK)
        xc = o_ref[pl.ds(r, NORM_ROW_CHUNK), :]
        ms = jnp.mean(xc * xc, axis=-1, keepdims=True)
        o_ref[pl.ds(r, NORM_ROW_CHUNK), :] = xc * lax.rsqrt(ms + EPS) * g_ref[...]
        return carry

    lax.fori_loop(0, o_ref.shape[0] // NORM_ROW_CHUNK, body, 0, unroll=2)


def out_proj_norm(o_main, o_mem, w, layer, x, g, *, name):
    M, k_main = o_main.shape
    k_mem = o_mem.shape[1]
    N = w.shape[-1]
    tm = FINAL_TM
    return pl.pallas_call(
        _out_proj_norm_kernel,
        grid=(M // tm,),
        in_specs=[
            pl.BlockSpec((tm, k_main), lambda i: (i, 0)),
            pl.BlockSpec((tm, k_mem), lambda i: (i, 0)),
            pl.BlockSpec((None, k_main + k_mem, N), lambda i: (layer, 0, 0)),
            pl.BlockSpec((tm, N), lambda i: (i, 0)),
            pl.BlockSpec((1, N), lambda i: (0, 0)),
        ],
        out_specs=pl.BlockSpec((tm, N), lambda i: (i, 0)),
        out_shape=jax.ShapeDtypeStruct((M, N), F32),
        compiler_params=_params(("parallel",)),
        name=name,
    )(o_main, o_mem, w, x, g)


def _retention_kernel(g_ref, q_ref, k_ref, v_ref, gate_ref, cos_ref, sin_ref, decay_ref, xi_ref, zeta_ref,
                      o_ref, state_ref):
    @pl.when(pl.program_id(1) == 0)
    def _reset():
        state_ref[...] = jnp.zeros_like(state_ref)

    def rotate(t, cos, sin):
        return t * cos + pltpu.roll(t, RET_DK // 2, axis=1) * sin

    for c in range(q_ref.shape[0] // RET_CHUNK):
        rows = pl.ds(c * RET_CHUNK, RET_CHUNK)
        cos = cos_ref[rows, :]
        sin = sin_ref[rows, :]
        for h in range(RET_HEADS):
            qk_cols = slice(h * RET_DK, (h + 1) * RET_DK)
            v_cols = slice(h * RET_DV, (h + 1) * RET_DV)
            q = rotate(q_ref[rows, qk_cols].astype(F32), cos, sin)
            k = rotate(k_ref[rows, qk_cols].astype(F32), cos, sin) * (RET_DK ** -0.5)
            v = v_ref[rows, v_cols]
            qb = q.astype(BF16)
            s = _dot_nt(qb, k.astype(BF16)) * decay_ref[h]
            state = state_ref[h]
            o = _dot(s.astype(BF16), v) + _dot(qb, state.astype(BF16)) * xi_ref[h]
            kz_t = (k * zeta_ref[h]).T.astype(BF16)
            state_ref[h] = g_ref[h] * state + _dot(kz_t, v)
            o = o * lax.rsqrt(jnp.mean(o * o, axis=-1, keepdims=True) + EPS)
            o_ref[rows, v_cols] = (o * _silu(gate_ref[rows, v_cols].astype(F32))).astype(o_ref.dtype)


def _rotary_tables(S):
    half = RET_DK // 2
    inv = 1.0 / (RET_THETA_BASE ** jnp.linspace(0.0, 1.0, half, dtype=F32))
    ang = jnp.arange(S, dtype=F32)[:, None] * inv[None, :]
    cos, sin = jnp.cos(ang), jnp.sin(ang)
    return jnp.concatenate([cos, cos], axis=-1), jnp.concatenate([-sin, sin], axis=-1)


def _decay_tables():
    H, C = RET_HEADS, RET_CHUNK
    log_g = jnp.log1p(-(2.0 ** (-5.0 - jnp.arange(H, dtype=F32))))
    pos = jnp.arange(C, dtype=F32)
    rel = pos[:, None] - pos[None, :]
    decay_in = jnp.where(rel >= 0, jnp.exp(jnp.maximum(rel, 0.0)[None] * log_g[:, None, None]), 0.0)
    xi = jnp.exp((pos[None, :] + 1.0) * log_g[:, None])
    zeta = jnp.exp((C - 1.0 - pos[None, :]) * log_g[:, None])
    g_chunk = jnp.exp(C * log_g)
    return g_chunk, decay_in, xi[:, :, None], zeta[:, :, None]


def retention_gated(proj, B, S):
    M = B * S
    tq = RET_TQ
    spb = S // tq
    g_chunk, decay_in, xi, zeta = _decay_tables()
    cos, sin = _rotary_tables(S)
    k_blk = 1
    v_blk = 2 * RET_QK_W // RET_V_W
    gate_blk = v_blk + 1
    row = lambda b, i: b * spb + i
    const3 = lambda b, i: (0, 0, 0)
    return pl.pallas_call(
        _retention_kernel,
        grid=(B, spb),
        in_specs=[
            pl.BlockSpec(memory_space=pltpu.SMEM),
            pl.BlockSpec((tq, RET_QK_W), lambda b, i: (row(b, i), 0)),
            pl.BlockSpec((tq, RET_QK_W), lambda b, i: (row(b, i), k_blk)),
            pl.BlockSpec((tq, RET_V_W), lambda b, i: (row(b, i), v_blk)),
            pl.BlockSpec((tq, RET_V_W), lambda b, i: (row(b, i), gate_blk)),
            pl.BlockSpec((tq, RET_DK), lambda b, i: (i, 0)),
            pl.BlockSpec((tq, RET_DK), lambda b, i: (i, 0)),
            pl.BlockSpec((RET_HEADS, RET_CHUNK, RET_CHUNK), const3),
            pl.BlockSpec((RET_HEADS, RET_CHUNK, 1), const3),
            pl.BlockSpec((RET_HEADS, RET_CHUNK, 1), const3),
        ],
        out_specs=pl.BlockSpec((tq, RET_V_W), lambda b, i: (row(b, i), 0)),
        out_shape=jax.ShapeDtypeStruct((M, RET_V_W), BF16),
        scratch_shapes=[pltpu.VMEM((RET_HEADS, RET_DK, RET_DV), F32)],
        compiler_params=_params(("parallel", "arbitrary")),
        name="retention",
    )(g_chunk, proj, proj, proj, proj, cos, sin, decay_in, xi, zeta)


def _mem_attn_kernel(q_ref, gate_ref, k_ref, v_ref, o_ref):
    for h in range(MEM_HEADS):
        cols = slice(h * MEM_DH, (h + 1) * MEM_DH)
        s = _dot_nt(q_ref[:, cols], k_ref[:, cols])
        p = jnp.exp2((s - jnp.max(s, axis=-1, keepdims=True)) * (MEM_DH ** -0.5 * math.log2(math.e)))
        l = jnp.sum(p, axis=-1, keepdims=True)
        o = _dot(p.astype(BF16), v_ref[:, cols]) * (1.0 / l)
        o_ref[:, cols] = (o * _silu(gate_ref[:, cols].astype(F32))).astype(o_ref.dtype)


def mem_attention_gated(proj, mem_kv, B, S, q_blk):
    M = B * S
    tq = MEM_TQ
    spb = S // tq
    row = lambda b, i: b * spb + i
    return pl.pallas_call(
        _mem_attn_kernel,
        grid=(B, spb),
        in_specs=[
            pl.BlockSpec((tq, MEM_W), lambda b, i: (row(b, i), q_blk)),
            pl.BlockSpec((tq, MEM_W), lambda b, i: (row(b, i), q_blk + 1)),
            pl.BlockSpec((MEM_LEN, MEM_W), lambda b, i: (b, 0)),
            pl.BlockSpec((MEM_LEN, MEM_W), lambda b, i: (b, 1)),
        ],
        out_specs=pl.BlockSpec((tq, MEM_W), lambda b, i: (row(b, i), 0)),
        out_shape=jax.ShapeDtypeStruct((M, MEM_W), BF16),
        compiler_params=_params(("parallel", "parallel")),
        name="mem_attention",
    )(proj, proj, mem_kv, mem_kv)


DIL_TOK = max(d for _, d in DIL_CONFIG) * DIL_BLOCK
DIL_KEYS = 2 * DIL_BLOCK


def _residue_rows(dil):
    per_class = DIL_TOK // dil
    return per_class, DIL_BLOCK + per_class


def _class_pieces(dil, r, pos0, count):
    per_perm = PERM_TOK // dil
    pieces = []
    while count:
        blk, off = divmod(pos0, per_perm)
        size = min(count, per_perm - off)
        pieces.append((blk * PERM_TOK + r * per_perm + off, size))
        pos0, count = pos0 + size, count - size
    return pieces


def _dilated_kernel(q0_ref, q1_ref, q2_ref, gate_ref, k0_ref, v0_ref, k1_ref, v1_ref, k2_ref, v2_ref, o_ref,
                    ke0_ref, ve0_ref, ke1_ref, ve1_ref, ke2_ref, ve2_ref, acc_ref, m_ref, l_ref):
    j = pl.program_id(2)
    q_refs = (q0_ref, q1_ref, q2_ref)
    kv_refs = ((k0_ref, v0_ref, ke0_ref, ve0_ref), (k1_ref, v1_ref, ke1_ref, ve1_ref), (k2_ref, v2_ref, ke2_ref, ve2_ref))

    @pl.when(j == 0)
    def _zero_halos():
        for (_, _, ke_ref, ve_ref), (_, dil) in zip(kv_refs, DIL_CONFIG):
            seg = _residue_rows(dil)[1]
            for r in range(dil):
                ke_ref[r * seg:r * seg + DIL_BLOCK, :] = jnp.zeros((DIL_BLOCK, DIL_DH), BF16)
                ve_ref[r * seg:r * seg + DIL_BLOCK, :] = jnp.zeros((DIL_BLOCK, DIL_DH), BF16)

    @pl.when(j > 0)
    def _carry_halos():
        for (_, _, ke_ref, ve_ref), (_, dil) in zip(kv_refs, DIL_CONFIG):
            per_class, seg = _residue_rows(dil)
            for r in range(dil):
                ke_ref[r * seg:r * seg + DIL_BLOCK, :] = ke_ref[r * seg + per_class:(r + 1) * seg, :]
                ve_ref[r * seg:r * seg + DIL_BLOCK, :] = ve_ref[r * seg + per_class:(r + 1) * seg, :]

    for (k_ref, v_ref, ke_ref, ve_ref), (_, dil) in zip(kv_refs, DIL_CONFIG):
        per_class, seg = _residue_rows(dil)
        for r in range(dil):
            dst = r * seg + DIL_BLOCK
            for start, size in _class_pieces(dil, r, 0, per_class):
                ke_ref[dst:dst + size, :] = k_ref[start:start + size, :]
                ve_ref[dst:dst + size, :] = v_ref[start:start + size, :]
                dst += size

    key_lead = (lax.broadcasted_iota(jnp.int32, (DIL_BLOCK, DIL_BLOCK), 1)
                - lax.broadcasted_iota(jnp.int32, (DIL_BLOCK, DIL_BLOCK), 0))
    first_min_lead = jnp.where(j == 0, DIL_BLOCK, 0)
    exp2_scale = DIL_DH ** -0.5 * math.log2(math.e)

    def attend(q, ke_ref, ve_ref, key0, prev_min_lead, tok, first_group, last_group):
        kc = ke_ref[pl.ds(key0, DIL_KEYS), :]
        vc = ve_ref[pl.ds(key0, DIL_KEYS), :]
        s = _dot_nt(q, kc)
        s_prev = jnp.where(key_lead >= prev_min_lead, s[:, :DIL_BLOCK], -1e30)
        s_cur = jnp.where(key_lead <= 0, s[:, DIL_BLOCK:], -1e30)
        m_blk = jnp.max(jnp.maximum(s_prev, s_cur), axis=-1, keepdims=True)
        if first_group:
            m_new = jnp.broadcast_to(m_blk, (DIL_BLOCK, DIL_DH))
        else:
            m_old = m_ref[tok, :]
            m_new = jnp.maximum(m_old, m_blk)
            alpha = jnp.exp2((m_old - m_new) * exp2_scale)
        p_prev = jnp.exp2((s_prev - m_new) * exp2_scale)
        p_cur = jnp.exp2((s_cur - m_new) * exp2_scale)
        l = jnp.sum(p_prev + p_cur, axis=-1, keepdims=True)
        acc = _dot(jnp.concatenate([p_prev, p_cur], axis=1).astype(BF16), vc)
        if not first_group:
            l = alpha * l_ref[tok, :] + l
            acc = alpha * acc_ref[tok, :] + acc
        if last_group:
            o = acc * (1.0 / l) * _silu(gate_ref[tok, :].astype(F32))
            o_ref[tok, :] = o.astype(o_ref.dtype)
        else:
            l_ref[tok, :] = jnp.broadcast_to(l, (DIL_BLOCK, DIL_DH))
            acc_ref[tok, :] = acc
            m_ref[tok, :] = m_new

    groups = sorted(range(len(DIL_CONFIG)), key=lambda g: -DIL_CONFIG[g][1])
    for group in groups:
        dil = DIL_CONFIG[group][1]
        per_class, seg = _residue_rows(dil)
        _, _, ke_ref, ve_ref = kv_refs[group]
        for r in range(dil):
            for blk in range(per_class // DIL_BLOCK):
                pieces = _class_pieces(dil, r, blk * DIL_BLOCK, DIL_BLOCK)
                q = jnp.concatenate([q_refs[group][start:start + size, :] for start, size in pieces], axis=0)
                if dil == 1:
                    tok = pl.ds(blk * DIL_BLOCK, DIL_BLOCK)
                else:
                    tok = pl.ds(blk * DIL_BLOCK * dil + r, DIL_BLOCK, stride=dil)
                attend(q, ke_ref, ve_ref, r * seg + blk * DIL_BLOCK, first_min_lead if blk == 0 else 0, tok,
                       group == groups[0], group == groups[-1])


def dilated_gated(proj, kv, B, S):
    assert S % DIL_TOK == 0
    nblk = S // DIL_TOK
    heads_per_group = DIL_W // DIL_DH

    def head_cols(group_col):
        return pl.BlockSpec((DIL_TOK, DIL_DH), lambda b, h, j: (b * nblk + j, group_col * heads_per_group + h))

    ext = lambda dil: pltpu.VMEM((dil * _residue_rows(dil)[1], DIL_DH), BF16)
    tok_f32 = pltpu.VMEM((DIL_TOK, DIL_DH), F32)
    return pl.pallas_call(
        _dilated_kernel,
        grid=(B, DIL_HEADS, nblk),
        in_specs=[head_cols(c) for c in range(4)] + [head_cols(c) for c in range(6)],
        out_specs=pl.BlockSpec((DIL_TOK, DIL_DH), lambda b, h, j: (b * nblk + j, h)),
        out_shape=jax.ShapeDtypeStruct((B * S, DIL_W), BF16),
        scratch_shapes=[ext(d) for _, d in DIL_CONFIG for _ in range(2)] + [tok_f32, tok_f32, tok_f32],
        compiler_params=_params(("arbitrary", "arbitrary", "arbitrary")),
        name="dilated_attention",
    )(proj, proj, proj, proj, kv, kv, kv, kv, kv, kv)


def kernel(x, mem, norm_a, w_in_a, w_out_a, norm_b, w_in_b, w_out_b, w_mem_kv, mem_norm_g, kv_norm_g, w_kv, final_norm_g):
    B, S, D = x.shape
    xr = x.reshape(B * S, D)
    mem_r = mem.reshape(B * mem.shape[1], D)

    w_in_a, w_out_a, w_in_b, w_out_b, w_mem_kv = (t.astype(BF16) for t in (w_in_a, w_out_a, w_in_b, w_out_b, w_mem_kv))
    w_kv = w_kv.astype(BF16)[None]
    norm_a = norm_a[:, None, :]
    norm_b = norm_b[:, None, :]
    mem_g = mem_norm_g[None, None, :]
    kv_g = kv_norm_g[None, None, :]

    dils = tuple(d for _, d in DIL_CONFIG)
    tiles_per_group = DIL_W // PROJ_TN
    group_copies = tuple(range(len(DIL_CONFIG)))
    in_b_copies = tuple(c for c in group_copies + (0,) * ((IN_W_B - len(DIL_CONFIG) * DIL_W) // DIL_W)
                        for _ in range(tiles_per_group))
    kv_copies = tuple(c for c in group_copies for _ in range(2 * tiles_per_group))

    kv = None
    for layer in range(DEPTH):
        mem_kv = norm_matmul(mem_r, mem_g, 0, w_mem_kv, layer, name=f"mem_kv_proj{layer}")
        if layer < N_A:
            proj = norm_matmul(xr, norm_a, layer, w_in_a, layer, name=f"in_proj_a{layer}")
            o_main = retention_gated(proj, B, S)
            o_mem = mem_attention_gated(proj, mem_kv, B, S, (IN_W_A - 2 * MEM_W) // MEM_W)
            xr = out_proj(o_main, o_mem, w_out_a, layer, xr, name=f"out_proj_a{layer}")
            if layer == N_A - 1:
                kv = norm_matmul(xr, kv_g, 0, w_kv, 0, name="shared_kv_proj", dils=dils, tile_copy=kv_copies)
        else:
            jb = layer - N_A
            proj = norm_matmul(xr, norm_b, jb, w_in_b, jb, name=f"in_proj_b{jb}", dils=dils, tile_copy=in_b_copies)
            o_main = dilated_gated(proj, kv, B, S)
            o_mem = mem_attention_gated(proj, mem_kv, B, S, (IN_W_B - 2 * MEM_W) // MEM_W)
            if layer < DEPTH - 1:
                xr = out_proj(o_main, o_mem, w_out_b, jb, xr, name=f"out_proj_b{jb}")
            else:
                xr = out_proj_norm(o_main, o_mem, w_out_b, jb, xr, final_norm_g[None, :], name="out_proj_final_norm")
    return xr.reshape(B, S, D)
```
